```python
import jax, jax.numpy as jnp
from jax import lax
import numpy as np

D_MODEL = 1024
BATCH = 8
SEQ = 4096
DEPTH = 1
DEC_BATCH = 32
DEC_SEQ = 4
PAST_LEN = 16384
PAGE_SIZE = 128

A_HEADS = 8
A_HD = 64
A_WIDTH = A_HEADS * A_HD
MOBA_BLOCK = 256
MOBA_TOPK = 3
MOBA_QCHUNK = 32
HG_HEADS = 4
HG_DK = 128
HG_DV = 128
HG_KWIDTH = HG_HEADS * HG_DK
HG_WIDTH = HG_HEADS * HG_DV
HG_CHUNK = 64
D_FF = 2816
CONV_W = 3
N_BRANCH = 2
IN_SIZES = [A_WIDTH, A_WIDTH, A_WIDTH, HG_KWIDTH, HG_KWIDTH, HG_WIDTH, HG_WIDTH, D_MODEL, D_MODEL]
IN_COLS = sum(IN_SIZES)
IN_SPLITS = [int(s) for s in np.cumsum(IN_SIZES)[:-1]]
EPS = 1e-6

kernel_name = "moba_hgrn2_gated_parallel_hybrid_step"


def rmsnorm(x, g):
    xf = x.astype(jnp.float32)
    y = xf * lax.rsqrt(jnp.mean(xf * xf, axis=-1, keepdims=True) + EPS)
    return (y * g.astype(jnp.float32)).astype(x.dtype)


def moba_attention(q, k_full, v_full, q_pos):
    B, Q, H, hd = q.shape
    nb = k_full.shape[1] // MOBA_BLOCK
    kblk = k_full.reshape(B, nb, MOBA_BLOCK, H, hd)
    vblk = v_full.reshape(B, nb, MOBA_BLOCK, H, hd)
    kmean = jnp.mean(kblk.astype(jnp.float32), axis=2)
    n_sel = min(MOBA_TOPK, nb)
    scale = A_HD ** -0.5
    qc = next(c for c in (MOBA_QCHUNK, 16, 8, 4, 2, 1) if Q % c == 0)
    n_chunks = Q // qc
    b_idx = jnp.arange(B)[:, None, None]
    h_idx = jnp.arange(H)[None, :, None]
    tpos = jnp.arange(MOBA_BLOCK, dtype=jnp.int32)

    def chunk(args):
        q_c, pos_c = args
        qf = q_c.astype(jnp.float32)
        own = pos_c[0] // MOBA_BLOCK
        gate = jnp.einsum('bqhd,bnhd->bhqn', qf, kmean)
        gate = jnp.where(jnp.arange(nb) < own, gate, -jnp.inf)
        _, sel = lax.top_k(gate, n_sel)
        scores, v_parts = [], []
        for j in range(n_sel):
            idx = sel[..., j]
            kb = kblk[b_idx, idx, :, h_idx, :]
            s = jnp.einsum('bqhd,bhqtd->bhqt', qf, kb.astype(jnp.float32)) * scale
            scores.append(jnp.where(j < own, s, -jnp.inf))
            v_parts.append(idx)
        k_own = lax.dynamic_slice_in_dim(k_full, own * MOBA_BLOCK, MOBA_BLOCK, axis=1)
        v_own = lax.dynamic_slice_in_dim(v_full, own * MOBA_BLOCK, MOBA_BLOCK, axis=1)
        s_own = jnp.einsum('bqhd,bthd->bhqt', qf, k_own.astype(jnp.float32)) * scale
        causal = (own * MOBA_BLOCK + tpos)[None, :] <= pos_c[:, None]
        scores.append(jnp.where(causal, s_own, -jnp.inf))
        p = jax.nn.softmax(jnp.concatenate(scores, axis=-1), axis=-1)
        out = jnp.einsum('bhqt,bthd->bqhd', p[..., n_sel * MOBA_BLOCK:], v_own.astype(jnp.float32))
        for j in range(n_sel):
            vb = vblk[b_idx, v_parts[j], :, h_idx, :]
            out = out + jnp.einsum('bhqt,bhqtd->bqhd', p[..., j * MOBA_BLOCK:(j + 1) * MOBA_BLOCK],
                                   vb.astype(jnp.float32))
        return out.astype(q.dtype)

    qs = q.reshape(B, n_chunks, qc, H, hd).transpose(1, 0, 2, 3, 4)
    ps = q_pos.reshape(n_chunks, qc)
    o = lax.map(chunk, (qs, ps))
    return o.transpose(1, 0, 2, 3, 4).reshape(B, Q, H * hd)


def hgrn2_recurrence(q, k, logf, v, S0):
    B, T, H, dk = q.shape
    dv = v.shape[-1]
    C = HG_CHUNK if T % HG_CHUNK == 0 else T
    n = T // C

    def split(a):
        return a.astype(jnp.float32).reshape(B, n, C, H, a.shape[-1]).transpose(1, 0, 3, 2, 4)

    tri = jnp.tril(jnp.ones((C, C), dtype=bool))

    def step(S, xs):
        qc, kc, gc, vc = xs
        b = jnp.cumsum(gc, axis=2)
        diff = jnp.where(tri[None, None, :, :, None], b[:, :, :, None, :] - b[:, :, None, :, :], -jnp.inf)
        A = jnp.einsum('bhtd,bhsd,bhtsd->bhts', qc, kc, jnp.exp(diff))
        o = jnp.einsum('bhts,bhsv->bhtv', A, vc) + jnp.einsum('bhtd,bhdv->bhtv', qc * jnp.exp(b), S)
        b_last = b[:, :, -1:, :]
        S_new = jnp.exp(b_last[:, :, 0, :])[..., None] * S + jnp.einsum('bhsd,bhsv->bhdv', kc * jnp.exp(b_last - b), vc)
        return S_new, o

    S_T, o = lax.scan(step, S0.astype(jnp.float32), (split(q), split(k), split(logf), split(v)))
    o = o.transpose(1, 0, 3, 2, 4).reshape(B, T, H, dv)
    return o, S_T


def decoder_layer(x, c, k_past, v_past, S0, conv_prev, lb,
                  norm1_g, norm2_g, w_ada, b_ada, w_in, hg_norm_g, w_a_out, w_b_out, w_o,
                  w_up, conv_w, conv_b, w_down):
    B, T, _ = x.shape
    pos0 = k_past.shape[1]
    mod = jnp.dot(jax.nn.silu(c), w_ada) + b_ada
    sh1, sc1, g1, sh2, sc2, g2 = jnp.split(mod[:, None, :], 6, axis=-1)

    h = rmsnorm(x, norm1_g) * (1 + sc1) + sh1
    proj = jnp.dot(h, w_in)
    aq, ak, av, hq, hf, hi, hg, ga, gb = jnp.split(proj, IN_SPLITS, axis=-1)

    aq = aq.reshape(B, T, A_HEADS, A_HD)
    ak = ak.reshape(B, T, A_HEADS, A_HD)
    av = av.reshape(B, T, A_HEADS, A_HD)
    L = pos0 + T
    nb = -(-L // MOBA_BLOCK)
    pad = jnp.zeros((B, nb * MOBA_BLOCK - L, A_HEADS, A_HD), ak.dtype)
    k_full = jnp.concatenate([k_past.astype(ak.dtype), ak, pad], axis=1)
    v_full = jnp.concatenate([v_past.astype(av.dtype), av, pad], axis=1)
    q_pos = pos0 + jnp.arange(T, dtype=jnp.int32)
    o_a = moba_attention(aq, k_full, v_full, q_pos)

    qh = jax.nn.silu(hq.astype(jnp.float32)).reshape(B, T, HG_HEADS, HG_DK)
    logf = jnp.logaddexp(jnp.log(lb), jnp.log1p(-lb) + jax.nn.log_sigmoid(hf.astype(jnp.float32)))
    kh = -jnp.expm1(logf)
    o_b, S_T = hgrn2_recurrence(qh, kh.reshape(B, T, HG_HEADS, HG_DK), logf.reshape(B, T, HG_HEADS, HG_DK),
                                hi.reshape(B, T, HG_HEADS, HG_DV), S0)
    o_b = rmsnorm(o_b, hg_norm_g.reshape(HG_HEADS, HG_DV)) * jax.nn.silu(hg.astype(jnp.float32)).reshape(B, T, HG_HEADS, HG_DV)
    o_b = o_b.reshape(B, T, HG_WIDTH).astype(x.dtype)

    y_a = jnp.dot(o_a, w_a_out)
    y_b = jnp.dot(o_b, w_b_out)
    mixed = jax.nn.sigmoid(ga) * y_a + jax.nn.sigmoid(gb) * y_b
    x = x + g1 * jnp.dot(mixed, w_o)

    h = rmsnorm(x, norm2_g) * (1 + sc2) + sh2
    u, v = jnp.split(jnp.dot(h, w_up), 2, axis=-1)
    u_ext = jnp.concatenate([conv_prev.astype(u.dtype), u], axis=1)
    u_c = conv_b + conv_w[0] * u_ext[:, 0:T]
    for j in range(1, CONV_W):
        u_c = u_c + conv_w[j] * u_ext[:, j:j + T]
    act = jax.nn.gelu(u_c, approximate=False) * v
    x = x + g2 * jnp.dot(act, w_down)
    new_conv = u_ext[:, -(CONV_W - 1):]
    return x, ak, av, S_T.astype(S0.dtype), new_conv


def setup_inputs(seed: int = 0) -> dict:
    key = jax.random.key(seed)
    ks = jax.random.split(key, 24)
    f32 = jnp.float32
    n_pages = PAST_LEN // PAGE_SIZE
    n_used = DEC_BATCH * n_pages
    n_phys = n_used + n_used // 4

    def nrm(k, shape, s):
        return jax.random.normal(k, shape, f32) * s

    return {
        "x_prompt": nrm(ks[0], (BATCH, SEQ, D_MODEL), 1.0),
        "x_sample": nrm(ks[1], (DEC_BATCH, DEC_SEQ, D_MODEL), 1.0),
        "cache_k": nrm(ks[2], (DEPTH, n_phys, PAGE_SIZE, A_HEADS, A_HD), 1.0),
        "cache_v": nrm(ks[3], (DEPTH, n_phys, PAGE_SIZE, A_HEADS, A_HD), 1.0),
        "state_hgrn": nrm(ks[4], (DEPTH, DEC_BATCH, HG_HEADS, HG_DK, HG_DV), 0.5),
        "state_conv": nrm(ks[5], (DEPTH, DEC_BATCH, CONV_W - 1, D_FF), 1.0),
        "page_table": jax.random.permutation(ks[6], n_phys)[:n_used].reshape(DEC_BATCH, n_pages).astype(jnp.int32),
        "c_prompt": nrm(ks[7], (BATCH, D_MODEL), 1.0),
        "c_sample": nrm(ks[8], (DEC_BATCH, D_MODEL), 1.0),
        "norm1_g": 1.0 + nrm(ks[9], (DEPTH, D_MODEL), 0.05),
        "norm2_g": 1.0 + nrm(ks[10], (DEPTH, D_MODEL), 0.05),
        "w_ada": nrm(ks[11], (DEPTH, D_MODEL, 6 * D_MODEL), 0.5 * D_MODEL ** -0.5),
        "b_ada": nrm(ks[12], (DEPTH, 6 * D_MODEL), 0.01),
        "w_in": nrm(ks[13], (DEPTH, D_MODEL, IN_COLS), D_MODEL ** -0.5),
        "hgrn_lb_logits": nrm(ks[14], (DEPTH + 1, HG_KWIDTH), 0.1),
        "hg_norm_g": 1.0 + nrm(ks[15], (DEPTH, HG_WIDTH), 0.05),
        "w_a_out": nrm(ks[16], (DEPTH, A_WIDTH, D_MODEL), A_WIDTH ** -0.5),
        "w_b_out": nrm(ks[17], (DEPTH, HG_WIDTH, D_MODEL), HG_WIDTH ** -0.5),
        "w_o": nrm(ks[18], (DEPTH, D_MODEL, D_MODEL), D_MODEL ** -0.5),
        "w_up": nrm(ks[19], (DEPTH, D_MODEL, 2 * D_FF), D_MODEL ** -0.5),
        "conv_w": nrm(ks[20], (DEPTH, CONV_W, D_FF), CONV_W ** -0.5),
        "conv_b": nrm(ks[21], (DEPTH, D_FF), 0.01),
        "w_down": nrm(ks[22], (DEPTH, D_FF, D_MODEL), D_FF ** -0.5),
        "final_g": 1.0 + nrm(ks[23], (D_MODEL,), 0.05),
    }


def reference(x_prompt, x_sample, cache_k, cache_v, state_hgrn, state_conv, page_table, c_prompt, c_sample,
              norm1_g, norm2_g, w_ada, b_ada, w_in, hgrn_lb_logits, hg_norm_g, w_a_out, w_b_out, w_o,
              w_up, conv_w, conv_b, w_down, final_g):
    Bp = x_prompt.shape[0]
    Bs = x_sample.shape[0]
    past_len = page_table.shape[1] * PAGE_SIZE
    lower = jnp.cumsum(jax.nn.softmax(hgrn_lb_logits.astype(jnp.float32), axis=0), axis=0)
    xp, xs = x_prompt, x_sample
    kp_l, vp_l, ks_l, vs_l, sp_l, ss_l, cp_l, cs_l = [], [], [], [], [], [], [], []
    for l in range(DEPTH):
        params = (norm1_g[l], norm2_g[l], w_ada[l], b_ada[l], w_in[l], hg_norm_g[l], w_a_out[l], w_b_out[l],
                  w_o[l], w_up[l], conv_w[l], conv_b[l], w_down[l])
        empty = jnp.zeros((Bp, 0, A_HEADS, A_HD), cache_k.dtype)
        xp, kp, vp, sp, cp = decoder_layer(
            xp, c_prompt, empty, empty,
            jnp.zeros((Bp, HG_HEADS, HG_DK, HG_DV), state_hgrn.dtype),
            jnp.zeros((Bp, CONV_W - 1, D_FF), state_conv.dtype), lower[l], *params)
        k_past = cache_k[l][page_table].reshape(Bs, past_len, A_HEADS, A_HD)
        v_past = cache_v[l][page_table].reshape(Bs, past_len, A_HEADS, A_HD)
        xs, kss, vss, ss, cs = decoder_layer(
            xs, c_sample, k_past, v_past, state_hgrn[l], state_conv[l], lower[l], *params)
        kp_l.append(kp); vp_l.append(vp); ks_l.append(kss); vs_l.append(vss)
        sp_l.append(sp); ss_l.append(ss); cp_l.append(cp); cs_l.append(cs)
    y_prompt = rmsnorm(xp, final_g)
    y_sample = rmsnorm(xs, final_g)
    return (y_prompt, y_sample, jnp.stack(kp_l), jnp.stack(vp_l), jnp.stack(ks_l), jnp.stack(vs_l),
            jnp.stack(sp_l), jnp.stack(ss_l), jnp.stack(cp_l), jnp.stack(cs_l))
```

```python
import functools

import jax
import jax.numpy as jnp
from jax import lax
from jax.experimental import pallas as pl
from jax.experimental.pallas import tpu as pltpu

F32 = jnp.float32
BF16 = jnp.bfloat16

A_HEADS = 8
A_HD = 64
A_WIDTH = A_HEADS * A_HD
MOBA_BLOCK = 256
MOBA_TOPK = 3
GATE_GROUP = 16
HG_HEADS = 4
HG_DK = 128
HG_DV = 128
HG_WIDTH = HG_HEADS * HG_DV
HG_CHUNK = 64
CONV_W = 3
PAGE_SIZE = 128
EPS = 1e-6
NEG_INF = float("-inf")

VMEM_LIMIT_BYTES = 56 * 1024 * 1024
ROW_TILE = 512
FF_TILE = 256
DECODE_PAGES_PER_STEP = 8


def _cparams(semantics):
    return pltpu.CompilerParams(dimension_semantics=semantics, vmem_limit_bytes=VMEM_LIMIT_BYTES)


def _nt(a, b):
    return lax.dot_general(a, b, (((1,), (1,)), ((), ())), preferred_element_type=F32)


def _tn(a, b):
    return lax.dot_general(a, b, (((0,), (0,)), ((), ())), preferred_element_type=F32)


def _mm(a, b):
    return jnp.dot(a, b, preferred_element_type=F32)


def _rms(x, g):
    return x * lax.rsqrt(jnp.mean(x * x, axis=-1, keepdims=True) + EPS) * g


def _ada_kernel(c_ref, w_ref, b_ref, o_ref):
    c = c_ref[...]
    s = c * jax.nn.sigmoid(c)
    o_ref[...] = _mm(s.astype(BF16), w_ref[...]) + b_ref[...]


def _ada_mod(c, w_bf, b):
    n, d = c.shape
    cols = w_bf.shape[1]
    tn = 1536
    assert cols % tn == 0
    return pl.pallas_call(
        _ada_kernel,
        grid=(cols // tn,),
        in_specs=[pl.BlockSpec((n, d), lambda j: (0, 0)),
                  pl.BlockSpec((d, tn), lambda j: (0, j)),
                  pl.BlockSpec((1, tn), lambda j: (0, j))],
        out_specs=pl.BlockSpec((n, tn), lambda j: (0, j)),
        out_shape=jax.ShapeDtypeStruct((n, cols), F32),
        compiler_params=_cparams(("arbitrary",)),
        name="ada_mod",
    )(c, w_bf, b)


def _mod_spec(per_row, tm, tiles_per_seq, d):
    if per_row:
        return pl.BlockSpec((None, tm, d), lambda i: (0, i, 0))
    return pl.BlockSpec((None, 1, d), lambda i: (i // tiles_per_seq, 0, 0))


def _inproj_kernel(x_ref, sh_ref, sc_ref, g_ref, w_ref, lbl_ref,
                   q_ref, k32_ref, v32_ref, kb_ref, vb_ref, qh_ref, lf_ref, kh_ref, hi_ref, hg_ref,
                   ga_ref, gb_ref, *, d_model):
    x = x_ref[...]
    h = _rms(x, g_ref[...]) * (1.0 + sc_ref[...]) + sh_ref[...]
    hb = h.astype(BF16)

    def proj(c0, width):
        return _mm(hb, w_ref[:, c0:c0 + width])

    w = A_WIDTH
    q = proj(0, w)
    q_ref[...] = (q * (A_HD ** -0.5)).astype(BF16)
    k = proj(w, w)
    k32_ref[...] = k
    kb_ref[...] = k.astype(BF16)
    v = proj(2 * w, w)
    v32_ref[...] = v
    vb_ref[...] = v.astype(BF16)
    c0 = 3 * w
    hq = proj(c0, w)
    qh_ref[...] = (hq * jax.nn.sigmoid(hq)).astype(BF16)
    hf = proj(c0 + w, w)
    lbl = lbl_ref[...]
    e = jnp.exp(lbl - jnp.max(lbl, axis=0, keepdims=True))
    lb = e[0:1, :] / jnp.sum(e, axis=0, keepdims=True)
    en = jnp.exp(-jnp.abs(hf))
    r = 1.0 / (1.0 + en)
    pos = hf >= 0.0
    sg = jnp.where(pos, r, en * r)
    sgn = jnp.where(pos, en * r, r)
    lf_ref[...] = jnp.log(lb + (1.0 - lb) * sg)
    kh_ref[...] = ((1.0 - lb) * sgn).astype(BF16)
    hi_ref[...] = proj(c0 + 2 * w, w).astype(BF16)
    hg = proj(c0 + 3 * w, w)
    hg_ref[...] = (hg * jax.nn.sigmoid(hg)).astype(BF16)
    c1 = c0 + 4 * w
    ga_ref[...] = jax.nn.sigmoid(proj(c1, d_model)).astype(BF16)
    gb_ref[...] = jax.nn.sigmoid(proj(c1 + d_model, d_model)).astype(BF16)


def _in_proj(x2d, sh1, sc1, norm_g, w_in_bf, lb_logits, *, per_row, seq_len):
    m, d = x2d.shape
    tm = min(ROW_TILE, m)
    assert m % tm == 0 and (per_row or seq_len % tm == 0)
    tps = max(seq_len // tm, 1)
    mod = _mod_spec(per_row, tm, tps, d)
    row = lambda width: pl.BlockSpec((tm, width), lambda i: (i, 0))
    full = lambda a: pl.BlockSpec(a.shape, lambda i: (0,) * a.ndim)
    w = A_WIDTH
    outs = [(w, BF16), (w, F32), (w, F32), (w, BF16), (w, BF16), (w, BF16), (w, F32), (w, BF16), (w, BF16),
            (w, BF16), (d, BF16), (d, BF16)]
    return pl.pallas_call(
        functools.partial(_inproj_kernel, d_model=d),
        grid=(m // tm,),
        in_specs=[row(d), mod, mod, full(norm_g), full(w_in_bf), full(lb_logits)],
        out_specs=[row(width) for width, _ in outs],
        out_shape=[jax.ShapeDtypeStruct((m, width), dt) for width, dt in outs],
        compiler_params=_cparams(("arbitrary",)),
        name="in_proj",
    )(x2d, sh1, sc1, norm_g, w_in_bf, lb_logits)


def _moba_prompt_kernel(q_ref, k_ref, v_ref, o_ref, km_ref, kmbd_ref, *, nb):
    i = pl.program_id(1)
    blk = MOBA_BLOCK
    gg = GATE_GROUP

    @pl.when(i == 0)
    def _():
        km_ref[...] = jnp.zeros_like(km_ref)
        for n in range(nb):
            kblk = k_ref[n * blk:(n + 1) * blk, :].astype(F32)
            km_ref[n:n + 1, :] = jnp.mean(kblk, axis=0, keepdims=True)
        km = km_ref[...]
        lane_head = lax.broadcasted_iota(jnp.int32, km.shape, 1) // A_HD
        for h in range(A_HEADS):
            kmbd_ref[h * gg:(h + 1) * gg, :] = jnp.where(lane_head == h, km, 0.0).astype(BF16)

    q = q_ref[...]
    gate = _nt(q, kmbd_ref[...])
    lane = lax.broadcasted_iota(jnp.int32, gate.shape, 1)
    n_id = lane % gg
    rank = jnp.zeros(gate.shape, jnp.int32)
    for d in range(1, gg):
        wrap = n_id + d >= gg
        other = jnp.where(wrap, pltpu.roll(gate, gg - d, axis=1), pltpu.roll(gate, 128 - d, axis=1))
        m_id = jnp.where(wrap, n_id + d - gg, n_id + d)
        beats = (m_id < i) & ((other > gate) | ((other == gate) & (m_id < n_id)))
        rank = rank + beats.astype(jnp.int32)
    sel = jnp.where((n_id < i) & (rank < MOBA_TOPK), 1.0, 0.0)

    row = lax.broadcasted_iota(jnp.int32, (blk, blk), 0)
    col = lax.broadcasted_iota(jnp.int32, (blk, blk), 1)
    own0 = pl.multiple_of(i * blk, blk)
    for h in range(A_HEADS):
        hs = slice(h * A_HD, (h + 1) * A_HD)
        qh = q[:, hs]
        s = _nt(qh, k_ref[pl.ds(own0, blk), hs])
        s = jnp.where(col <= row, s, NEG_INF)
        m0 = jnp.max(s, axis=-1, keepdims=True)
        p = jnp.exp(s - m0)
        l0 = jnp.sum(p, axis=-1, keepdims=True)
        acc0 = _mm(p.astype(BF16), v_ref[pl.ds(own0, blk), hs])

        def body(j, carry, h=h, hs=hs, qh=qh):
            m, l, acc = carry
            j0 = pl.multiple_of(j * blk, blk)
            s = _nt(qh, k_ref[pl.ds(j0, blk), hs])
            picked = jnp.max(jnp.where(lane == h * gg + j, sel, 0.0), axis=-1, keepdims=True)
            s = jnp.where(picked > 0.0, s, NEG_INF)
            m_new = jnp.maximum(m, jnp.max(s, axis=-1, keepdims=True))
            alpha = jnp.exp(m - m_new)
            p = jnp.exp(s - m_new)
            l = alpha * l + jnp.sum(p, axis=-1, keepdims=True)
            acc = alpha * acc + _mm(p.astype(BF16), v_ref[pl.ds(j0, blk), hs])
            return m_new, l, acc

        _, l, acc = lax.fori_loop(0, i, body, (m0, l0, acc0))
        o_ref[:, hs] = (acc / l).astype(BF16)


def _moba_prompt(q, k, v):
    b, t, w = q.shape
    assert t % MOBA_BLOCK == 0
    nb = t // MOBA_BLOCK
    assert nb <= GATE_GROUP
    tile = pl.BlockSpec((None, MOBA_BLOCK, w), lambda bi, i: (bi, i, 0))
    seq = pl.BlockSpec((None, t, w), lambda bi, i: (bi, 0, 0))
    return pl.pallas_call(
        functools.partial(_moba_prompt_kernel, nb=nb),
        grid=(b, nb),
        in_specs=[tile, seq, seq],
        out_specs=tile,
        out_shape=jax.ShapeDtypeStruct((b, t, w), BF16),
        scratch_shapes=[pltpu.VMEM((GATE_GROUP, w), F32), pltpu.VMEM((A_HEADS * GATE_GROUP, w), BF16)],
        compiler_params=_cparams(("arbitrary", "arbitrary")),
        name="moba_prompt",
    )(q, k, v)


def _moba_decode_kernel(pt_ref, qbd_ref, kn_ref, vn_ref, *refs, n_steps, pp, n_blocks, tq):
    k_refs = refs[:pp]
    v_refs = refs[pp:2 * pp]
    o_ref = refs[2 * pp]
    s_ref, g_ref, m_ref, l_ref, acc_ref = refs[2 * pp + 1:]
    del pt_ref
    s_id = pl.program_id(1)
    rows = tq * A_HEADS
    pages_per_block = MOBA_BLOCK // PAGE_SIZE
    qbd = qbd_ref[...]
    lane = lax.broadcasted_iota(jnp.int32, (rows, 128), 1)

    @pl.when(s_id == 0)
    def _():
        g_ref[...] = jnp.zeros_like(g_ref)

    @pl.when(s_id < n_steps)
    def _():
        part = None
        for pi in range(pp):
            page = s_id * pp + pi
            sc = _nt(qbd, k_refs[pi][...].astype(BF16))
            s_ref[:, pl.ds(pl.multiple_of(page * PAGE_SIZE, PAGE_SIZE), PAGE_SIZE)] = sc
            rs = jnp.sum(sc, axis=-1, keepdims=True)
            part = rs if pi % pages_per_block == 0 else part + rs
            if pi % pages_per_block == pages_per_block - 1:
                n = page // pages_per_block
                g_ref[...] = jnp.where(lane == n, part * (1.0 / MOBA_BLOCK), g_ref[...])

    @pl.when(s_id == n_steps - 1)
    def _():
        g = jnp.where(lane < n_blocks, g_ref[...], NEG_INF)
        sel = jnp.zeros(g.shape, F32)
        for _ in range(min(MOBA_TOPK, n_blocks)):
            mx = jnp.max(g, axis=-1, keepdims=True)
            idx = jnp.min(jnp.where(g == mx, lane, 128), axis=-1, keepdims=True)
            pick = lane == idx
            sel = jnp.where(pick, 1.0, sel)
            g = jnp.where(pick, NEG_INF, g)

        def mask_block(n, m):
            picked = jnp.max(jnp.where(lane == n, sel, 0.0), axis=-1, keepdims=True)
            b0 = pl.multiple_of(n * MOBA_BLOCK, MOBA_BLOCK)
            sb = jnp.where(picked > 0.0, s_ref[:, pl.ds(b0, MOBA_BLOCK)], NEG_INF)
            s_ref[:, pl.ds(b0, MOBA_BLOCK)] = sb
            return jnp.maximum(m, jnp.max(sb, axis=-1, keepdims=True))

        m_past = lax.fori_loop(0, n_blocks, mask_block, jnp.full((rows, 1), NEG_INF, F32))
        sn = _nt(qbd, kn_ref[...])
        qi = lax.broadcasted_iota(jnp.int32, sn.shape, 0) // A_HEADS
        sn = jnp.where(lane <= qi, sn, NEG_INF)
        m = jnp.maximum(m_past, jnp.max(sn, axis=-1, keepdims=True))
        pn = jnp.exp(sn - m)
        m_ref[...] = jnp.broadcast_to(m, m_ref.shape)
        l_ref[...] = jnp.broadcast_to(jnp.sum(pn, axis=-1, keepdims=True), l_ref.shape)
        acc_ref[...] = _mm(pn.astype(BF16), vn_ref[...])

    @pl.when(s_id >= n_steps)
    def _():
        m = m_ref[:, 0:1]
        lsum = jnp.zeros((rows, 1), F32)
        acc = jnp.zeros(acc_ref.shape, F32)
        for pi in range(pp):
            page = (s_id - n_steps) * pp + pi
            sc = s_ref[:, pl.ds(pl.multiple_of(page * PAGE_SIZE, PAGE_SIZE), PAGE_SIZE)]
            p = jnp.exp(sc - m)
            lsum = lsum + jnp.sum(p, axis=-1, keepdims=True)
            acc = acc + _mm(p.astype(BF16), v_refs[pi][...].astype(BF16))
        l_ref[...] = l_ref[...] + lsum
        acc_ref[...] = acc_ref[...] + acc

    @pl.when(s_id == 2 * n_steps - 1)
    def _():
        o = acc_ref[...] / l_ref[:, 0:1]
        r_head = lax.broadcasted_iota(jnp.int32, o.shape, 0) % A_HEADS
        l_head = lax.broadcasted_iota(jnp.int32, o.shape, 1) // A_HD
        o = jnp.where(r_head == l_head, o, 0.0)
        o_ref[...] = jnp.zeros_like(o_ref)
        o_ref[0:tq, :] = jnp.sum(o.reshape(tq, A_HEADS, o.shape[-1]), axis=1)


def _moba_decode(q, k_new, v_new, cache_k, cache_v, page_table):
    b, tq, w = q.shape
    n_pages = page_table.shape[1]
    past = n_pages * PAGE_SIZE
    assert past % MOBA_BLOCK == 0 and tq <= PAGE_SIZE and tq <= MOBA_BLOCK
    n_blocks = past // MOBA_BLOCK
    assert n_blocks <= 128
    pp = DECODE_PAGES_PER_STEP
    assert n_pages % pp == 0 and pp % (MOBA_BLOCK // PAGE_SIZE) == 0
    n_steps = n_pages // pp
    rows = tq * A_HEADS
    head_of_lane = jnp.arange(w, dtype=jnp.int32) // A_HD
    head_mask = (head_of_lane[None, :] == jnp.arange(A_HEADS, dtype=jnp.int32)[:, None])
    qbd = jnp.where(head_mask[None, None], q[:, :, None, :], jnp.zeros((), q.dtype)).reshape(b, rows, w)
    pad = ((0, 0), (0, PAGE_SIZE - tq), (0, 0))
    kn = jnp.pad(k_new, pad)
    vn = jnp.pad(v_new, pad)

    def k_map(pi):
        return lambda bi, s, pt: (pt[bi, jnp.minimum(s, n_steps - 1) * pp + pi], 0, 0)

    def v_map(pi):
        return lambda bi, s, pt: (pt[bi, jnp.maximum(s - n_steps, 0) * pp + pi], 0, 0)

    per_b = lambda r: pl.BlockSpec((None, r, w), lambda bi, s, pt: (bi, 0, 0))
    page = lambda imap: pl.BlockSpec((None, PAGE_SIZE, w), imap)
    out_rows = 8 * pl.cdiv(tq, 8)
    grid_spec = pltpu.PrefetchScalarGridSpec(
        num_scalar_prefetch=1,
        grid=(b, 2 * n_steps),
        in_specs=[per_b(rows), per_b(PAGE_SIZE), per_b(PAGE_SIZE)]
        + [page(k_map(pi)) for pi in range(pp)] + [page(v_map(pi)) for pi in range(pp)],
        out_specs=per_b(out_rows),
        scratch_shapes=[pltpu.VMEM((rows, past), F32), pltpu.VMEM((rows, 128), F32),
                        pltpu.VMEM((rows, 128), F32), pltpu.VMEM((rows, 128), F32),
                        pltpu.VMEM((rows, w), F32)],
    )
    out = pl.pallas_call(
        functools.partial(_moba_decode_kernel, n_steps=n_steps, pp=pp, n_blocks=n_blocks, tq=tq),
        grid_spec=grid_spec,
        out_shape=jax.ShapeDtypeStruct((b, out_rows, w), F32),
        compiler_params=_cparams(("arbitrary", "arbitrary")),
        name="moba_decode",
    )(page_table, qbd, kn, vn, *([cache_k] * pp), *([cache_v] * pp))
    return out[:, :tq, :]


def _hgrn_kernel(qh_ref, lf_ref, kh_ref, v_ref, hg_ref, s0_ref, g_ref, o_ref, st_out_ref, st_ref, *, n_chunks):
    t = pl.program_id(1)
    c = HG_CHUNK

    @pl.when(t == 0)
    def _():
        for h in range(HG_HEADS):
            st_ref[h] = s0_ref[h].T

    row = lax.broadcasted_iota(jnp.int32, (c, HG_DK), 0)
    tril = lax.broadcasted_iota(jnp.int32, (c, c), 1) <= lax.broadcasted_iota(jnp.int32, (c, c), 0)
    for ci in range(n_chunks):
        rs = slice(ci * c, (ci + 1) * c)
        for h in range(HG_HEADS):
            ks = slice(h * HG_DK, (h + 1) * HG_DK)
            vs = slice(h * HG_DV, (h + 1) * HG_DV)
            b = lf_ref[rs, ks]
            sh = 1
            while sh < c:
                b = b + jnp.where(row >= sh, pltpu.roll(b, sh, axis=0), 0.0)
                sh *= 2
            ref = b[c // 2 - 1:c // 2, :]
            last = b[c - 1:c, :]
            qh = qh_ref[rs, ks].astype(F32)
            kh = kh_ref[rs, ks].astype(F32)
            v = v_ref[rs, vs]
            qe = (qh * jnp.exp(b - ref)).astype(BF16)
            ke = (kh * jnp.exp(ref - b)).astype(BF16)
            a = jnp.where(tril, _nt(qe, ke), 0.0)
            qs = (qh * jnp.exp(b)).astype(BF16)
            kd = (kh * jnp.exp(last - b)).astype(BF16)
            st = st_ref[h]
            o = _mm(a.astype(BF16), v) + _nt(qs, st.astype(BF16))
            st_ref[h] = st * jnp.exp(last) + _tn(v, kd)
            y = _rms(o, g_ref[:, vs]) * hg_ref[rs, vs].astype(F32)
            o_ref[rs, vs] = y.astype(BF16)

    @pl.when(t == pl.num_programs(1) - 1)
    def _():
        for h in range(HG_HEADS):
            st_out_ref[h] = st_ref[h].T


def _hgrn(qh, lf, kh, v, hg, s0, norm_g):
    b, t, w = qh.shape
    assert t % HG_CHUNK == 0
    rows = ROW_TILE if t % ROW_TILE == 0 else HG_CHUNK
    seq = pl.BlockSpec((None, rows, w), lambda bi, ti: (bi, ti, 0))
    state = pl.BlockSpec((None, HG_HEADS, HG_DK, HG_DV), lambda bi, ti: (bi, 0, 0, 0))
    return pl.pallas_call(
        functools.partial(_hgrn_kernel, n_chunks=rows // HG_CHUNK),
        grid=(b, t // rows),
        in_specs=[seq, seq, seq, seq, seq, state, pl.BlockSpec(norm_g.shape, lambda bi, ti: (0, 0))],
        out_specs=[seq, state],
        out_shape=[jax.ShapeDtypeStruct((b, t, w), BF16), jax.ShapeDtypeStruct(s0.shape, F32)],
        scratch_shapes=[pltpu.VMEM((HG_HEADS, HG_DV, HG_DK), F32)],
        compiler_params=_cparams(("arbitrary", "arbitrary")),
        name="hgrn",
    )(qh, lf, kh, v, hg, s0, norm_g)


def _mix_kernel(oa_ref, ob_ref, ga_ref, gb_ref, x_ref, g1_ref, sh2_ref, sc2_ref, n2_ref, wa_ref, wb_ref, wo_ref,
                x1_ref, h2_ref):
    ya = _mm(oa_ref[...], wa_ref[...])
    yb = _mm(ob_ref[...], wb_ref[...])
    mixed = ga_ref[...].astype(F32) * ya + gb_ref[...].astype(F32) * yb
    x1 = x_ref[...] + g1_ref[...] * _mm(mixed.astype(BF16), wo_ref[...])
    x1_ref[...] = x1
    h2_ref[...] = (_rms(x1, n2_ref[...]) * (1.0 + sc2_ref[...]) + sh2_ref[...]).astype(BF16)


def _mix_out(oa, ob, ga, gb, x2d, g1, sh2, sc2, norm2_g, wa, wb, wo, *, per_row, seq_len):
    m, d = x2d.shape
    tm = min(ROW_TILE, m)
    tps = max(seq_len // tm, 1)
    mod = _mod_spec(per_row, tm, tps, d)
    row = lambda width: pl.BlockSpec((tm, width), lambda i: (i, 0))
    full = lambda a: pl.BlockSpec(a.shape, lambda i: (0,) * a.ndim)
    return pl.pallas_call(
        _mix_kernel,
        grid=(m // tm,),
        in_specs=[row(A_WIDTH), row(HG_WIDTH), row(d), row(d), row(d), mod, mod, mod, full(norm2_g),
                  full(wa), full(wb), full(wo)],
        out_specs=[row(d), row(d)],
        out_shape=[jax.ShapeDtypeStruct((m, d), F32), jax.ShapeDtypeStruct((m, d), BF16)],
        compiler_params=_cparams(("arbitrary",)),
        name="mix_out",
    )(oa, ob, ga, gb, x2d, g1, sh2, sc2, norm2_g, wa, wb, wo)


def _gelu(x):
    return 0.5 * x * (1.0 + lax.erf(x * (2.0 ** -0.5)))


def _ffn_body(h2_ref, x1_ref, g2_ref, wup_ref, cw_ref, cb_ref, wdn_ref, fg_ref, y_ref, acc_ref, prev_rows, emit_u):
    tm, d_ff = h2_ref.shape[0], cw_ref.shape[1]
    h2 = h2_ref[...]
    acc_ref[...] = jnp.zeros_like(acc_ref)
    for ft in range(d_ff // FF_TILE):
        fs = slice(ft * FF_TILE, (ft + 1) * FF_TILE)
        u = _mm(h2, wup_ref[:, fs])
        v = _mm(h2, wup_ref[:, d_ff + ft * FF_TILE:d_ff + (ft + 1) * FF_TILE])
        um1, um2 = prev_rows(u, fs)
        uc = cb_ref[:, fs] + cw_ref[0:1, fs] * um2 + cw_ref[1:2, fs] * um1 + cw_ref[2:3, fs] * u
        act = (_gelu(uc) * v).astype(BF16)
        acc_ref[...] += _mm(act, wdn_ref[fs, :])
        emit_u(u, fs)
    x2 = x1_ref[...] + g2_ref[...] * acc_ref[...]
    y_ref[...] = _rms(x2, fg_ref[...])


def _ffn_seq_kernel(h2_ref, x1_ref, g2_ref, cp_ref, wup_ref, cw_ref, cb_ref, wdn_ref, fg_ref,
                    y_ref, nc_ref, acc_ref, carry_ref):
    tm = h2_ref.shape[0]

    @pl.when(pl.program_id(1) == 0)
    def _():
        carry_ref[...] = cp_ref[...]

    row = lax.broadcasted_iota(jnp.int32, (tm, FF_TILE), 0)

    def prev_rows(u, fs):
        c0 = carry_ref[0:1, fs]
        c1 = carry_ref[1:2, fs]
        um1 = jnp.where(row == 0, c1, pltpu.roll(u, 1, axis=0))
        um2 = jnp.where(row == 0, c0, jnp.where(row == 1, c1, pltpu.roll(u, 2, axis=0)))
        return um1, um2

    def emit_u(u, fs):
        tail = u[tm - (CONV_W - 1):tm, :]
        carry_ref[:, fs] = tail
        nc_ref[:, fs] = tail

    _ffn_body(h2_ref, x1_ref, g2_ref, wup_ref, cw_ref, cb_ref, wdn_ref, fg_ref, y_ref, acc_ref, prev_rows, emit_u)


def _ffn_short_kernel(h2_ref, x1_ref, g2_ref, p1_ref, p2_ref, wup_ref, cw_ref, cb_ref, wdn_ref, fg_ref,
                      y_ref, u_ref, acc_ref, *, seq_len):
    tm = h2_ref.shape[0]
    tpos = lax.broadcasted_iota(jnp.int32, (tm, FF_TILE), 0) % seq_len

    def prev_rows(u, fs):
        um1 = jnp.where(tpos >= 1, pltpu.roll(u, 1, axis=0), p1_ref[:, fs])
        um2 = jnp.where(tpos >= 2, pltpu.roll(u, 2, axis=0), p2_ref[:, fs])
        return um1, um2

    def emit_u(u, fs):
        u_ref[:, fs] = u

    _ffn_body(h2_ref, x1_ref, g2_ref, wup_ref, cw_ref, cb_ref, wdn_ref, fg_ref, y_ref, acc_ref, prev_rows, emit_u)


def _ffn_seq(h2, x1, g2, conv_prev, wup, cw, cb, wdn, fg, *, batch, seq_len):
    m, d = x1.shape
    d_ff = cw.shape[1]
    tm = min(ROW_TILE, seq_len)
    assert seq_len % tm == 0 and d_ff % FF_TILE == 0 and tm >= CONV_W - 1
    tps = seq_len // tm
    row = lambda width: pl.BlockSpec((tm, width), lambda bi, ti: (bi * tps + ti, 0))
    full = lambda a: pl.BlockSpec(a.shape, lambda bi, ti: (0,) * a.ndim)
    per_b = lambda r, width: pl.BlockSpec((None, r, width), lambda bi, ti: (bi, 0, 0))
    return pl.pallas_call(
        _ffn_seq_kernel,
        grid=(batch, tps),
        in_specs=[row(d), row(d), per_b(1, d), per_b(CONV_W - 1, d_ff), full(wup), full(cw), full(cb), full(wdn),
                  full(fg)],
        out_specs=[row(d), per_b(CONV_W - 1, d_ff)],
        out_shape=[jax.ShapeDtypeStruct((m, d), F32), jax.ShapeDtypeStruct((batch, CONV_W - 1, d_ff), F32)],
        scratch_shapes=[pltpu.VMEM((tm, d), F32), pltpu.VMEM((CONV_W - 1, d_ff), F32)],
        compiler_params=_cparams(("arbitrary", "arbitrary")),
        name="conv_ffn",
    )(h2, x1, g2, conv_prev, wup, cw, cb, wdn, fg)


def _ffn_short(h2, x1, g2_rows, conv_prev, wup, cw, cb, wdn, fg, *, batch, seq_len):
    m, d = x1.shape
    d_ff = cw.shape[1]
    assert CONV_W == 3 and seq_len >= CONV_W - 1
    zeros = lambda n: jnp.zeros((batch, n, d_ff), F32)
    p1 = jnp.concatenate([conv_prev[:, 1:2], zeros(seq_len - 1)], axis=1).reshape(m, d_ff)
    p2 = jnp.concatenate([conv_prev[:, 0:2], zeros(seq_len - 2)], axis=1).reshape(m, d_ff)
    tm = m
    row = lambda width: pl.BlockSpec((tm, width), lambda i: (i, 0))
    full = lambda a: pl.BlockSpec(a.shape, lambda i: (0,) * a.ndim)
    y, u = pl.pallas_call(
        functools.partial(_ffn_short_kernel, seq_len=seq_len),
        grid=(m // tm,),
        in_specs=[row(d), row(d), pl.BlockSpec((None, tm, d), lambda i: (0, i, 0)), row(d_ff), row(d_ff),
                  full(wup), full(cw), full(cb), full(wdn), full(fg)],
        out_specs=[row(d), row(d_ff)],
        out_shape=[jax.ShapeDtypeStruct((m, d), F32), jax.ShapeDtypeStruct((m, d_ff), F32)],
        scratch_shapes=[pltpu.VMEM((tm, d), F32)],
        compiler_params=_cparams(("arbitrary",)),
        name="conv_ffn_short",
    )(h2, x1, g2_rows, p1, p2, wup, cw, cb, wdn, fg)
    new_conv = u.reshape(batch, seq_len, d_ff)[:, seq_len - (CONV_W - 1):, :]
    return y, new_conv


def kernel(x_prompt, x_sample, cache_k, cache_v, state_hgrn, state_conv, page_table, c_prompt, c_sample,
           norm1_g, norm2_g, w_ada, b_ada, w_in, hgrn_lb_logits, hg_norm_g, w_a_out, w_b_out, w_o,
           w_up, conv_w, conv_b, w_down, final_g):
    depth = w_in.shape[0]
    assert depth == 1 and hgrn_lb_logits.shape[0] == depth + 1
    bp, tp, d = x_prompt.shape
    bs, ts, _ = x_sample.shape
    d_ff = conv_w.shape[-1]
    w = A_WIDTH

    w_ada_bf = w_ada[0].astype(BF16)
    w_in_bf = w_in[0].astype(BF16)
    wa_bf = w_a_out[0].astype(BF16)
    wb_bf = w_b_out[0].astype(BF16)
    wo_bf = w_o[0].astype(BF16)
    wup_bf = w_up[0].astype(BF16)
    wdn_bf = w_down[0].astype(BF16)
    n1 = norm1_g[0].reshape(1, d)
    n2 = norm2_g[0].reshape(1, d)
    fg = final_g.reshape(1, d)
    hgn = hg_norm_g[0].reshape(1, HG_WIDTH)
    cw = conv_w[0]
    cb = conv_b[0].reshape(1, d_ff)

    mod = _ada_mod(jnp.concatenate([c_prompt, c_sample], axis=0), w_ada_bf, b_ada[0].reshape(1, 6 * d))

    mp = bp * tp
    modp = [mod[:bp, i * d:(i + 1) * d].reshape(bp, 1, d) for i in range(6)]
    xp = x_prompt.reshape(mp, d)
    (q, k32, v32, kb, vb, qh, lf, kh, hi, hg, ga, gb) = _in_proj(
        xp, modp[0], modp[1], n1, w_in_bf, hgrn_lb_logits, per_row=False, seq_len=tp)
    seq3 = lambda a: a.reshape(bp, tp, w)
    oa = _moba_prompt(seq3(q), seq3(kb), seq3(vb)).reshape(mp, w)
    ob, sp = _hgrn(seq3(qh), seq3(lf), seq3(kh), seq3(hi), seq3(hg),
                   jnp.zeros((bp, HG_HEADS, HG_DK, HG_DV), F32), hgn)
    x1, h2 = _mix_out(oa, ob.reshape(mp, w), ga, gb, xp, modp[2], modp[3], modp[4], n2, wa_bf, wb_bf, wo_bf,
                      per_row=False, seq_len=tp)
    yp, cp = _ffn_seq(h2, x1, modp[5], jnp.zeros((bp, CONV_W - 1, d_ff), F32), wup_bf, cw, cb, wdn_bf, fg,
                      batch=bp, seq_len=tp)

    ms = bs * ts
    mods_rows = jnp.repeat(mod[bp:], ts, axis=0)
    mods = [mods_rows[:, i * d:(i + 1) * d].reshape(1, ms, d) for i in range(6)]
    xs = x_sample.reshape(ms, d)
    (q_s, k32_s, v32_s, kb_s, vb_s, qh_s, lf_s, kh_s, hi_s, hg_s, ga_s, gb_s) = _in_proj(
        xs, mods[0], mods[1], n1, w_in_bf, hgrn_lb_logits, per_row=True, seq_len=ts)
    seq3s = lambda a: a.reshape(bs, ts, w)
    n_phys = cache_k.shape[1]
    oa_s = _moba_decode(seq3s(q_s), seq3s(kb_s), seq3s(vb_s),
                        cache_k[0].reshape(n_phys, PAGE_SIZE, w), cache_v[0].reshape(n_phys, PAGE_SIZE, w),
                        page_table)
    t_pad = HG_CHUNK * pl.cdiv(ts, HG_CHUNK)
    padt = lambda a: jnp.pad(seq3s(a), ((0, 0), (0, t_pad - ts), (0, 0)))
    ob_s, ss = _hgrn(padt(qh_s), padt(lf_s), padt(kh_s), padt(hi_s), padt(hg_s), state_hgrn[0], hgn)
    ob_s = ob_s[:, :ts, :].reshape(ms, w)
    x1_s, h2_s = _mix_out(oa_s.reshape(ms, w).astype(BF16), ob_s, ga_s, gb_s, xs, mods[2], mods[3], mods[4], n2,
                          wa_bf, wb_bf, wo_bf, per_row=True, seq_len=ts)
    ys, cs = _ffn_short(h2_s, x1_s, mods[5], state_conv[0], wup_bf, cw, cb, wdn_bf, fg, batch=bs, seq_len=ts)

    kv_p = lambda a: a.reshape(1, bp, tp, A_HEADS, A_HD)
    kv_s = lambda a: a.reshape(1, bs, ts, A_HEADS, A_HD)
    return (yp.reshape(bp, tp, d), ys.reshape(bs, ts, d), kv_p(k32), kv_p(v32), kv_s(k32_s), kv_s(v32_s),
            sp[None], ss[None], cp[None], cs[None])
```

```python
import functools

import jax
import jax.numpy as jnp
from jax import lax
from jax.experimental import pallas as pl
from jax.experimental.pallas import tpu as pltpu

F32 = jnp.float32
BF16 = jnp.bfloat16

A_HEADS = 8
A_HD = 64
A_WIDTH = A_HEADS * A_HD
MOBA_BLOCK = 256
MOBA_TOPK = 3
GATE_GROUP = 16
HG_HEADS = 4
HG_DK = 128
HG_DV = 128
HG_WIDTH = HG_HEADS * HG_DV
HG_CHUNK = 64
CONV_W = 3
PAGE_SIZE = 128
EPS = 1e-6
NEG_INF = float("-inf")
MASK_BIAS = -1e30
LOG2E = 1.4426950408889634

VMEM_LIMIT_BYTES = 56 * 1024 * 1024
ROW_TILE = 512
FF_TILE = 256
DECODE_PAGES_PER_STEP = 16
SCORE_LOOKAHEAD = 4


def _cparams(semantics, flags=None):
    return pltpu.CompilerParams(dimension_semantics=semantics, vmem_limit_bytes=VMEM_LIMIT_BYTES, flags=flags)


def _nt(a, b):
    return lax.dot_general(a, b, (((1,), (1,)), ((), ())), preferred_element_type=F32)


def _tn(a, b):
    return lax.dot_general(a, b, (((0,), (0,)), ((), ())), preferred_element_type=F32)


def _mm(a, b):
    return jnp.dot(a, b, preferred_element_type=F32)


def _rms(x, g):
    return x * lax.rsqrt(jnp.mean(x * x, axis=-1, keepdims=True) + EPS) * g


def _ada_kernel(c_ref, w_ref, b_ref, o_ref):
    c = c_ref[...]
    s = c * jax.nn.sigmoid(c)
    o_ref[...] = _mm(s.astype(BF16), w_ref[...]) + b_ref[...]


def _ada_mod(c, w_bf, b):
    n, d = c.shape
    cols = w_bf.shape[1]
    tn = 1536
    assert cols % tn == 0
    return pl.pallas_call(
        _ada_kernel,
        grid=(cols // tn,),
        in_specs=[pl.BlockSpec((n, d), lambda j: (0, 0)),
                  pl.BlockSpec((d, tn), lambda j: (0, j)),
                  pl.BlockSpec((1, tn), lambda j: (0, j))],
        out_specs=pl.BlockSpec((n, tn), lambda j: (0, j)),
        out_shape=jax.ShapeDtypeStruct((n, cols), F32),
        compiler_params=_cparams(("arbitrary",)),
        name="ada_mod",
    )(c, w_bf, b)


def _mod_spec(per_row, tm, tiles_per_seq, d):
    if per_row:
        return pl.BlockSpec((None, tm, d), lambda i: (0, i, 0))
    return pl.BlockSpec((None, 1, d), lambda i: (i // tiles_per_seq, 0, 0))


def _inproj_kernel(x_ref, sh_ref, sc_ref, g_ref, w_ref, lbl_ref,
                   q_ref, k32_ref, v32_ref, kb_ref, vb_ref, qh_ref, lf_ref, kh_ref, hi_ref, hg_ref,
                   ga_ref, gb_ref, *, d_model, kv_transposed):
    x = x_ref[...]
    h = _rms(x, g_ref[...]) * (1.0 + sc_ref[...]) + sh_ref[...]
    hb = h.astype(BF16)

    def proj(c0, width):
        return _mm(hb, w_ref[:, c0:c0 + width])

    w = A_WIDTH
    q = proj(0, w)
    q_ref[...] = (q * (A_HD ** -0.5 * LOG2E)).astype(BF16)
    k = proj(w, w)
    kb_ref[...] = k.astype(BF16)
    v = proj(2 * w, w)
    if kv_transposed:
        k32_ref[...] = k.T
        vt = v.T
        v32_ref[...] = vt
        vb_ref[...] = vt.astype(BF16)
    else:
        k32_ref[...] = k
        v32_ref[...] = v
        vb_ref[...] = v.astype(BF16)
    c0 = 3 * w
    hq = proj(c0, w)
    qh_ref[...] = (hq * jax.nn.sigmoid(hq)).astype(BF16)
    hf = proj(c0 + w, w)
    lbl = lbl_ref[...]
    e = jnp.exp(lbl - jnp.max(lbl, axis=0, keepdims=True))
    lb = e[0:1, :] / jnp.sum(e, axis=0, keepdims=True)
    en = jnp.exp(-jnp.abs(hf))
    r = 1.0 / (1.0 + en)
    pos = hf >= 0.0
    sg = jnp.where(pos, r, en * r)
    sgn = jnp.where(pos, en * r, r)
    lf_ref[...] = jnp.log(lb + (1.0 - lb) * sg)
    kh_ref[...] = ((1.0 - lb) * sgn).astype(BF16)
    hi_ref[...] = proj(c0 + 2 * w, w).astype(BF16)
    hg = proj(c0 + 3 * w, w)
    hg_ref[...] = (hg * jax.nn.sigmoid(hg)).astype(BF16)
    c1 = c0 + 4 * w
    ga_ref[...] = jax.nn.sigmoid(proj(c1, d_model)).astype(BF16)
    gb_ref[...] = jax.nn.sigmoid(proj(c1 + d_model, d_model)).astype(BF16)


def _in_proj(x2d, sh1, sc1, norm_g, w_in_bf, lb_logits, *, per_row, seq_len):
    m, d = x2d.shape
    tm = min(ROW_TILE, m)
    assert m % tm == 0 and (per_row or seq_len % tm == 0)
    tps = max(seq_len // tm, 1)
    mod = _mod_spec(per_row, tm, tps, d)
    row = lambda width: pl.BlockSpec((tm, width), lambda i: (i, 0))
    full = lambda a: pl.BlockSpec(a.shape, lambda i: (0,) * a.ndim)
    w = A_WIDTH
    outs = [(w, BF16), (w, F32), (w, F32), (w, BF16), (w, BF16), (w, BF16), (w, F32), (w, BF16), (w, BF16),
            (w, BF16), (d, BF16), (d, BF16)]
    out_specs = [row(width) for width, _ in outs]
    out_shape = [jax.ShapeDtypeStruct((m, width), dt) for width, dt in outs]
    kv_transposed = not per_row
    if kv_transposed:
        for idx in (1, 2, 4):
            out_specs[idx] = pl.BlockSpec((None, w, tm), lambda i: (i // tps, 0, i % tps))
            out_shape[idx] = jax.ShapeDtypeStruct((m // seq_len, w, seq_len), outs[idx][1])
    return pl.pallas_call(
        functools.partial(_inproj_kernel, d_model=d, kv_transposed=kv_transposed),
        grid=(m // tm,),
        in_specs=[row(d), mod, mod, full(norm_g), full(w_in_bf), full(lb_logits)],
        out_specs=out_specs,
        out_shape=out_shape,
        compiler_params=_cparams(("arbitrary",)),
        name="in_proj",
    )(x2d, sh1, sc1, norm_g, w_in_bf, lb_logits)


def _moba_prompt_kernel(q_ref, k_ref, vt_ref, o_ref, km_ref, kmbd_ref, kaug_ref, qaug_ref, m_ref, l_ref, acc_ref,
                        *, nb):
    i = pl.program_id(1)
    blk = MOBA_BLOCK
    gg = GATE_GROUP
    lane = lax.broadcasted_iota(jnp.int32, (blk, 128), 1)

    def head_lanes(x, h):
        part = x[:, (h // 2) * 128:(h // 2 + 1) * 128]
        return pltpu.roll(part, A_HD, axis=1) if h % 2 else part

    @pl.when(i == 0)
    def _():
        km_ref[...] = jnp.zeros_like(km_ref)
        for n in range(nb):
            rs = slice(n * blk, (n + 1) * blk)
            kf = k_ref[rs, :].astype(F32)
            km_ref[n:n + 1, :] = jnp.mean(kf, axis=0, keepdims=True)
            onehot = jnp.where(lane == A_HD + n, 1.0, 0.0)
            for h in range(A_HEADS):
                kaug_ref[rs, h * 128:(h + 1) * 128] = jnp.where(lane < A_HD, head_lanes(kf, h), onehot).astype(BF16)
        km = km_ref[...]
        lane_head = lax.broadcasted_iota(jnp.int32, km.shape, 1) // A_HD
        for h in range(A_HEADS):
            kmbd_ref[h * gg:(h + 1) * gg, :] = jnp.where(lane_head == h, km, 0.0).astype(BF16)

    q = q_ref[...]
    gate = _nt(q, kmbd_ref[...])
    n_id = lane % gg
    rank = jnp.zeros(gate.shape, jnp.int32)
    for d in range(1, gg):
        wrap = n_id + d >= gg
        other = jnp.where(wrap, pltpu.roll(gate, gg - d, axis=1), pltpu.roll(gate, 128 - d, axis=1))
        m_id = jnp.where(wrap, n_id + d - gg, n_id + d)
        beats = (m_id < i) & ((other > gate) | ((other == gate) & (m_id < n_id)))
        rank = rank + beats.astype(jnp.int32)
    bias = jnp.where((n_id < i) & (rank >= MOBA_TOPK), MASK_BIAS, 0.0)
    qf = q.astype(F32)
    for h in range(A_HEADS):
        bias_h = pltpu.roll(bias, (A_HD - h * gg) % 128, axis=1)
        qa = jnp.where(lane < A_HD, head_lanes(qf, h), jnp.where(lane < A_HD + gg, bias_h, 0.0))
        qaug_ref[h] = qa.astype(BF16)

    def scores(j0, h):
        return _nt(kaug_ref[pl.ds(j0, blk), h * 128:(h + 1) * 128], qaug_ref[h])

    own0 = pl.multiple_of(i * blk, blk)
    causal = (lax.broadcasted_iota(jnp.int32, (blk, blk), 0) <= lax.broadcasted_iota(jnp.int32, (blk, blk), 1))

    def own_scores(h):
        return jnp.where(causal, scores(own0, h), NEG_INF)

    for h in range(A_HEADS):
        m_ref[h, 0:1, :] = jnp.max(own_scores(h), axis=0, keepdims=True)

    def max_body(j, carry):
        j0 = pl.multiple_of(j * blk, blk)
        for h in range(A_HEADS):
            m_ref[h, 0:1, :] = jnp.maximum(m_ref[h, 0:1, :], jnp.max(scores(j0, h), axis=0, keepdims=True))
        return carry

    lax.fori_loop(0, i, max_body, 0)

    def accumulate(j0, s, h, first):
        hr = slice(h * A_HD, (h + 1) * A_HD)
        p = jnp.exp2(s - m_ref[h, 0:1, :])
        psum = jnp.sum(p, axis=0, keepdims=True)
        pv = _mm(vt_ref[hr, pl.ds(j0, blk)], p.astype(BF16))
        l_ref[h, 0:1, :] = psum if first else l_ref[h, 0:1, :] + psum
        acc_ref[hr, :] = pv if first else acc_ref[hr, :] + pv

    def all_heads(j0, score_fn, first):
        pending = [score_fn(h) for h in range(SCORE_LOOKAHEAD)]
        for h in range(A_HEADS):
            if h + SCORE_LOOKAHEAD < A_HEADS:
                pending.append(score_fn(h + SCORE_LOOKAHEAD))
            accumulate(j0, pending.pop(0), h, first)

    all_heads(own0, own_scores, True)

    def acc_body(j, carry):
        j0 = pl.multiple_of(j * blk, blk)
        all_heads(j0, functools.partial(scores, j0), False)
        return carry

    lax.fori_loop(0, i, acc_body, 0)
    for h in range(A_HEADS):
        hr = slice(h * A_HD, (h + 1) * A_HD)
        acc_ref[hr, :] = acc_ref[hr, :] / l_ref[h, 0:1, :]
    o_ref[...] = acc_ref[...].T.astype(BF16)


def _moba_prompt(q, k, vt):
    b, t, w = q.shape
    assert t % MOBA_BLOCK == 0
    nb = t // MOBA_BLOCK
    assert nb <= GATE_GROUP
    tile = pl.BlockSpec((None, MOBA_BLOCK, w), lambda bi, i: (bi, i, 0))
    return pl.pallas_call(
        functools.partial(_moba_prompt_kernel, nb=nb),
        grid=(b, nb),
        in_specs=[tile, pl.BlockSpec((None, t, w), lambda bi, i: (bi, 0, 0)),
                  pl.BlockSpec((None, w, t), lambda bi, i: (bi, 0, 0))],
        out_specs=tile,
        out_shape=jax.ShapeDtypeStruct((b, t, w), BF16),
        scratch_shapes=[pltpu.VMEM((GATE_GROUP, w), F32), pltpu.VMEM((A_HEADS * GATE_GROUP, w), BF16),
                        pltpu.VMEM((t, A_HEADS * 128), BF16),
                        pltpu.VMEM((A_HEADS, MOBA_BLOCK, 128), BF16),
                        pltpu.VMEM((A_HEADS, 8, MOBA_BLOCK), F32), pltpu.VMEM((A_HEADS, 8, MOBA_BLOCK), F32),
                        pltpu.VMEM((w, MOBA_BLOCK), F32)],
        compiler_params=_cparams(("arbitrary", "arbitrary")),
        name="moba_prompt",
    )(q, k, vt)


def _moba_decode_kernel(pt_ref, qbd_ref, kn_ref, vn_ref, *refs, n_steps, pp, n_blocks, tq):
    k_refs = refs[:pp]
    v_refs = refs[pp:2 * pp]
    o_ref = refs[2 * pp]
    s_ref, g_ref, m_ref, l_ref, acc_ref = refs[2 * pp + 1:]
    del pt_ref
    s_id = pl.program_id(1)
    rows = tq * A_HEADS
    pages_per_block = MOBA_BLOCK // PAGE_SIZE
    qbd = qbd_ref[...]
    lane = lax.broadcasted_iota(jnp.int32, (rows, 128), 1)

    @pl.when(s_id == 0)
    def _():
        g_ref[...] = jnp.zeros_like(g_ref)

    @pl.when(s_id < n_steps)
    def _():
        part = None
        for pi in range(pp):
            page = s_id * pp + pi
            sc = _mm(qbd, k_refs[pi][...].astype(BF16))
            s_ref[:, pl.ds(pl.multiple_of(page * PAGE_SIZE, PAGE_SIZE), PAGE_SIZE)] = sc
            rs = jnp.sum(sc, axis=-1, keepdims=True)
            part = rs if pi % pages_per_block == 0 else part + rs
            if pi % pages_per_block == pages_per_block - 1:
                n = page // pages_per_block
                g_ref[...] = jnp.where(lane == n, part * (1.0 / MOBA_BLOCK), g_ref[...])

    @pl.when(s_id == n_steps - 1)
    def _():
        g = jnp.where(lane < n_blocks, g_ref[...], NEG_INF)
        sel = jnp.zeros(g.shape, F32)
        for _ in range(min(MOBA_TOPK, n_blocks)):
            mx = jnp.max(g, axis=-1, keepdims=True)
            idx = jnp.min(jnp.where(g == mx, lane, 128), axis=-1, keepdims=True)
            pick = lane == idx
            sel = jnp.where(pick, 1.0, sel)
            g = jnp.where(pick, NEG_INF, g)

        def mask_block(n, m):
            picked = jnp.max(jnp.where(lane == n, sel, 0.0), axis=-1, keepdims=True)
            b0 = pl.multiple_of(n * MOBA_BLOCK, MOBA_BLOCK)
            sb = jnp.where(picked > 0.0, s_ref[:, pl.ds(b0, MOBA_BLOCK)], NEG_INF)
            s_ref[:, pl.ds(b0, MOBA_BLOCK)] = sb
            return jnp.maximum(m, jnp.max(sb, axis=-1, keepdims=True))

        m_past = lax.fori_loop(0, n_blocks, mask_block, jnp.full((rows, 1), NEG_INF, F32),
                               unroll=next(u for u in (8, 4, 2, 1) if n_blocks % u == 0))
        sn = _nt(qbd, kn_ref[...])
        qi = lax.broadcasted_iota(jnp.int32, sn.shape, 0) // A_HEADS
        sn = jnp.where(lane <= qi, sn, NEG_INF)
        m = jnp.maximum(m_past, jnp.max(sn, axis=-1, keepdims=True))
        pn = jnp.exp2(sn - m)
        m_ref[...] = jnp.broadcast_to(m, m_ref.shape)
        l_ref[...] = jnp.broadcast_to(jnp.sum(pn, axis=-1, keepdims=True), l_ref.shape)
        acc_ref[...] = _mm(pn.astype(BF16), vn_ref[...])

    @pl.when(s_id >= n_steps)
    def _():
        m = m_ref[:, 0:1]
        lsum = jnp.zeros((rows, 1), F32)
        acc = jnp.zeros(acc_ref.shape, F32)
        for pi in range(pp):
            page = (s_id - n_steps) * pp + pi
            sc = s_ref[:, pl.ds(pl.multiple_of(page * PAGE_SIZE, PAGE_SIZE), PAGE_SIZE)]
            p = jnp.exp2(sc - m)
            lsum = lsum + jnp.sum(p, axis=-1, keepdims=True)
            acc = acc + _nt(p.astype(BF16), v_refs[pi][...].astype(BF16))
        l_ref[...] = l_ref[...] + lsum
        acc_ref[...] = acc_ref[...] + acc

    @pl.when(s_id == 2 * n_steps - 1)
    def _():
        o = acc_ref[...] / l_ref[:, 0:1]
        r_head = lax.broadcasted_iota(jnp.int32, o.shape, 0) % A_HEADS
        l_head = lax.broadcasted_iota(jnp.int32, o.shape, 1) // A_HD
        o = jnp.where(r_head == l_head, o, 0.0)
        o_ref[...] = jnp.zeros_like(o_ref)
        o_ref[0:tq, :] = jnp.sum(o.reshape(tq, A_HEADS, o.shape[-1]), axis=1)


def _moba_decode(q, k_new, v_new, cache_kt, cache_vt, page_table):
    b, tq, w = q.shape
    n_pages = page_table.shape[1]
    past = n_pages * PAGE_SIZE
    assert past % MOBA_BLOCK == 0 and tq <= PAGE_SIZE and tq <= MOBA_BLOCK
    n_blocks = past // MOBA_BLOCK
    assert n_blocks <= 128
    pp = DECODE_PAGES_PER_STEP
    assert n_pages % pp == 0 and pp % (MOBA_BLOCK // PAGE_SIZE) == 0
    n_steps = n_pages // pp
    rows = tq * A_HEADS
    head_of_lane = jnp.arange(w, dtype=jnp.int32) // A_HD
    head_mask = (head_of_lane[None, :] == jnp.arange(A_HEADS, dtype=jnp.int32)[:, None])
    qbd = jnp.where(head_mask[None, None], q[:, :, None, :], jnp.zeros((), q.dtype)).reshape(b, rows, w)
    pad = ((0, 0), (0, PAGE_SIZE - tq), (0, 0))
    kn = jnp.pad(k_new, pad)
    vn = jnp.pad(v_new, pad)

    def k_map(pi):
        return lambda bi, s, pt: (pt[bi, jnp.minimum(s, n_steps - 1) * pp + pi], 0, 0)

    def v_map(pi):
        return lambda bi, s, pt: (pt[bi, jnp.maximum(s - n_steps, 0) * pp + pi], 0, 0)

    per_b = lambda r: pl.BlockSpec((None, r, w), lambda bi, s, pt: (bi, 0, 0))
    page = lambda imap: pl.BlockSpec((None, w, PAGE_SIZE), imap)
    out_rows = 8 * pl.cdiv(tq, 8)
    grid_spec = pltpu.PrefetchScalarGridSpec(
        num_scalar_prefetch=1,
        grid=(b, 2 * n_steps),
        in_specs=[per_b(rows), per_b(PAGE_SIZE), per_b(PAGE_SIZE)]
        + [page(k_map(pi)) for pi in range(pp)] + [page(v_map(pi)) for pi in range(pp)],
        out_specs=per_b(out_rows),
        scratch_shapes=[pltpu.VMEM((rows, past), F32), pltpu.VMEM((rows, 128), F32),
                        pltpu.VMEM((rows, 128), F32), pltpu.VMEM((rows, 128), F32),
                        pltpu.VMEM((rows, w), F32)],
    )
    out = pl.pallas_call(
        functools.partial(_moba_decode_kernel, n_steps=n_steps, pp=pp, n_blocks=n_blocks, tq=tq),
        grid_spec=grid_spec,
        out_shape=jax.ShapeDtypeStruct((b, out_rows, w), F32),
        compiler_params=_cparams(("arbitrary", "arbitrary")),
        name="moba_decode",
    )(page_table, qbd, kn, vn, *([cache_kt] * pp), *([cache_vt] * pp))
    return out[:, :tq, :]


def _hgrn_kernel(qh_ref, lf_ref, kh_ref, v_ref, hg_ref, s0_ref, g_ref, o_ref, st_out_ref, st_ref, *, n_chunks):
    t = pl.program_id(1)
    c = HG_CHUNK

    @pl.when(t == 0)
    def _():
        for h in range(HG_HEADS):
            st_ref[h] = s0_ref[h].T

    row = lax.broadcasted_iota(jnp.int32, (c, HG_DK), 0)
    tril = lax.broadcasted_iota(jnp.int32, (c, c), 1) <= lax.broadcasted_iota(jnp.int32, (c, c), 0)
    for ci in range(n_chunks):
        rs = slice(ci * c, (ci + 1) * c)
        for h in range(HG_HEADS):
            ks = slice(h * HG_DK, (h + 1) * HG_DK)
            vs = slice(h * HG_DV, (h + 1) * HG_DV)
            b = lf_ref[rs, ks]
            sh = 1
            while sh < c:
                b = b + jnp.where(row >= sh, pltpu.roll(b, sh, axis=0), 0.0)
                sh *= 2
            ref = b[c // 2 - 1:c // 2, :]
            last = b[c - 1:c, :]
            qh = qh_ref[rs, ks].astype(F32)
            kh = kh_ref[rs, ks].astype(F32)
            v = v_ref[rs, vs]
            qe = (qh * jnp.exp(b - ref)).astype(BF16)
            ke = (kh * jnp.exp(ref - b)).astype(BF16)
            a = jnp.where(tril, _nt(qe, ke), 0.0)
            qs = (qh * jnp.exp(b)).astype(BF16)
            kd = (kh * jnp.exp(last - b)).astype(BF16)
            st = st_ref[h]
            o = _mm(a.astype(BF16), v) + _nt(qs, st.astype(BF16))
            st_ref[h] = st * jnp.exp(last) + _tn(v, kd)
            y = _rms(o, g_ref[:, vs]) * hg_ref[rs, vs].astype(F32)
            o_ref[rs, vs] = y.astype(BF16)

    @pl.when(t == pl.num_programs(1) - 1)
    def _():
        for h in range(HG_HEADS):
            st_out_ref[h] = st_ref[h].T


def _hgrn(qh, lf, kh, v, hg, s0, norm_g):
    b, t, w = qh.shape
    assert t % HG_CHUNK == 0
    rows = ROW_TILE if t % ROW_TILE == 0 else HG_CHUNK
    seq = pl.BlockSpec((None, rows, w), lambda bi, ti: (bi, ti, 0))
    state = pl.BlockSpec((None, HG_HEADS, HG_DK, HG_DV), lambda bi, ti: (bi, 0, 0, 0))
    return pl.pallas_call(
        functools.partial(_hgrn_kernel, n_chunks=rows // HG_CHUNK),
        grid=(b, t // rows),
        in_specs=[seq, seq, seq, seq, seq, state, pl.BlockSpec(norm_g.shape, lambda bi, ti: (0, 0))],
        out_specs=[seq, state],
        out_shape=[jax.ShapeDtypeStruct((b, t, w), BF16), jax.ShapeDtypeStruct(s0.shape, F32)],
        scratch_shapes=[pltpu.VMEM((HG_HEADS, HG_DV, HG_DK), F32)],
        compiler_params=_cparams(("arbitrary", "arbitrary")),
        name="hgrn",
    )(qh, lf, kh, v, hg, s0, norm_g)


def _mix_kernel(oa_ref, ob_ref, ga_ref, gb_ref, x_ref, g1_ref, sh2_ref, sc2_ref, n2_ref, wa_ref, wb_ref, wo_ref,
                x1_ref, h2_ref):
    ya = _mm(oa_ref[...], wa_ref[...])
    yb = _mm(ob_ref[...], wb_ref[...])
    mixed = ga_ref[...].astype(F32) * ya + gb_ref[...].astype(F32) * yb
    x1 = x_ref[...] + g1_ref[...] * _mm(mixed.astype(BF16), wo_ref[...])
    x1_ref[...] = x1
    h2_ref[...] = (_rms(x1, n2_ref[...]) * (1.0 + sc2_ref[...]) + sh2_ref[...]).astype(BF16)


def _mix_out(oa, ob, ga, gb, x2d, g1, sh2, sc2, norm2_g, wa, wb, wo, *, per_row, seq_len):
    m, d = x2d.shape
    tm = min(ROW_TILE, m)
    tps = max(seq_len // tm, 1)
    mod = _mod_spec(per_row, tm, tps, d)
    row = lambda width: pl.BlockSpec((tm, width), lambda i: (i, 0))
    full = lambda a: pl.BlockSpec(a.shape, lambda i: (0,) * a.ndim)
    return pl.pallas_call(
        _mix_kernel,
        grid=(m // tm,),
        in_specs=[row(A_WIDTH), row(HG_WIDTH), row(d), row(d), row(d), mod, mod, mod, full(norm2_g),
                  full(wa), full(wb), full(wo)],
        out_specs=[row(d), row(d)],
        out_shape=[jax.ShapeDtypeStruct((m, d), F32), jax.ShapeDtypeStruct((m, d), BF16)],
        compiler_params=_cparams(("arbitrary",)),
        name="mix_out",
    )(oa, ob, ga, gb, x2d, g1, sh2, sc2, norm2_g, wa, wb, wo)


def _gelu(x):
    return 0.5 * x * (1.0 + lax.erf(x * (2.0 ** -0.5)))


def _ffn_body(h2_ref, x1_ref, g2_ref, wup_ref, cw_ref, cb_ref, wdn_ref, fg_ref, y_ref, acc_ref, prev_rows, emit_u):
    tm, d_ff = h2_ref.shape[0], cw_ref.shape[1]
    h2 = h2_ref[...]
    acc_ref[...] = jnp.zeros_like(acc_ref)
    for ft in range(d_ff // FF_TILE):
        fs = slice(ft * FF_TILE, (ft + 1) * FF_TILE)
        u = _mm(h2, wup_ref[:, fs])
        v = _mm(h2, wup_ref[:, d_ff + ft * FF_TILE:d_ff + (ft + 1) * FF_TILE])
        um1, um2 = prev_rows(u, fs)
        uc = cb_ref[:, fs] + cw_ref[0:1, fs] * um2 + cw_ref[1:2, fs] * um1 + cw_ref[2:3, fs] * u
        act = (_gelu(uc) * v).astype(BF16)
        acc_ref[...] += _mm(act, wdn_ref[fs, :])
        emit_u(u, fs)
    x2 = x1_ref[...] + g2_ref[...] * acc_ref[...]
    y_ref[...] = _rms(x2, fg_ref[...])


def _ffn_seq_kernel(h2_ref, x1_ref, g2_ref, cp_ref, wup_ref, cw_ref, cb_ref, wdn_ref, fg_ref,
                    y_ref, nc_ref, acc_ref, carry_ref):
    tm = h2_ref.shape[0]

    @pl.when(pl.program_id(1) == 0)
    def _():
        carry_ref[...] = cp_ref[...]

    row = lax.broadcasted_iota(jnp.int32, (tm, FF_TILE), 0)

    def prev_rows(u, fs):
        c0 = carry_ref[0:1, fs]
        c1 = carry_ref[1:2, fs]
        um1 = jnp.where(row == 0, c1, pltpu.roll(u, 1, axis=0))
        um2 = jnp.where(row == 0, c0, jnp.where(row == 1, c1, pltpu.roll(u, 2, axis=0)))
        return um1, um2

    def emit_u(u, fs):
        tail = u[tm - (CONV_W - 1):tm, :]
        carry_ref[:, fs] = tail
        nc_ref[:, fs] = tail

    _ffn_body(h2_ref, x1_ref, g2_ref, wup_ref, cw_ref, cb_ref, wdn_ref, fg_ref, y_ref, acc_ref, prev_rows, emit_u)


def _ffn_short_kernel(h2_ref, x1_ref, g2_ref, p1_ref, p2_ref, wup_ref, cw_ref, cb_ref, wdn_ref, fg_ref,
                      y_ref, u_ref, acc_ref, *, seq_len):
    tm = h2_ref.shape[0]
    tpos = lax.broadcasted_iota(jnp.int32, (tm, FF_TILE), 0) % seq_len

    def prev_rows(u, fs):
        um1 = jnp.where(tpos >= 1, pltpu.roll(u, 1, axis=0), p1_ref[:, fs])
        um2 = jnp.where(tpos >= 2, pltpu.roll(u, 2, axis=0), p2_ref[:, fs])
        return um1, um2

    def emit_u(u, fs):
        u_ref[:, fs] = u

    _ffn_body(h2_ref, x1_ref, g2_ref, wup_ref, cw_ref, cb_ref, wdn_ref, fg_ref, y_ref, acc_ref, prev_rows, emit_u)


def _ffn_seq(h2, x1, g2, conv_prev, wup, cw, cb, wdn, fg, *, batch, seq_len):
    m, d = x1.shape
    d_ff = cw.shape[1]
    tm = min(ROW_TILE, seq_len)
    assert seq_len % tm == 0 and d_ff % FF_TILE == 0 and tm >= CONV_W - 1
    tps = seq_len // tm
    row = lambda width: pl.BlockSpec((tm, width), lambda bi, ti: (bi * tps + ti, 0))
    full = lambda a: pl.BlockSpec(a.shape, lambda bi, ti: (0,) * a.ndim)
    per_b = lambda r, width: pl.BlockSpec((None, r, width), lambda bi, ti: (bi, 0, 0))
    return pl.pallas_call(
        _ffn_seq_kernel,
        grid=(batch, tps),
        in_specs=[row(d), row(d), per_b(1, d), per_b(CONV_W - 1, d_ff), full(wup), full(cw), full(cb), full(wdn),
                  full(fg)],
        out_specs=[row(d), per_b(CONV_W - 1, d_ff)],
        out_shape=[jax.ShapeDtypeStruct((m, d), F32), jax.ShapeDtypeStruct((batch, CONV_W - 1, d_ff), F32)],
        scratch_shapes=[pltpu.VMEM((tm, d), F32), pltpu.VMEM((CONV_W - 1, d_ff), F32)],
        compiler_params=_cparams(("arbitrary", "arbitrary")),
        name="conv_ffn",
    )(h2, x1, g2, conv_prev, wup, cw, cb, wdn, fg)


def _ffn_short(h2, x1, g2_rows, conv_prev, wup, cw, cb, wdn, fg, *, batch, seq_len):
    m, d = x1.shape
    d_ff = cw.shape[1]
    assert CONV_W == 3 and seq_len >= CONV_W - 1
    zeros = lambda n: jnp.zeros((batch, n, d_ff), F32)
    p1 = jnp.concatenate([conv_prev[:, 1:2], zeros(seq_len - 1)], axis=1).reshape(m, d_ff)
    p2 = jnp.concatenate([conv_prev[:, 0:2], zeros(seq_len - 2)], axis=1).reshape(m, d_ff)
    tm = m
    row = lambda width: pl.BlockSpec((tm, width), lambda i: (i, 0))
    full = lambda a: pl.BlockSpec(a.shape, lambda i: (0,) * a.ndim)
    y, u = pl.pallas_call(
        functools.partial(_ffn_short_kernel, seq_len=seq_len),
        grid=(m // tm,),
        in_specs=[row(d), row(d), pl.BlockSpec((None, tm, d), lambda i: (0, i, 0)), row(d_ff), row(d_ff),
                  full(wup), full(cw), full(cb), full(wdn), full(fg)],
        out_specs=[row(d), row(d_ff)],
        out_shape=[jax.ShapeDtypeStruct((m, d), F32), jax.ShapeDtypeStruct((m, d_ff), F32)],
        scratch_shapes=[pltpu.VMEM((tm, d), F32)],
        compiler_params=_cparams(("arbitrary",)),
        name="conv_ffn_short",
    )(h2, x1, g2_rows, p1, p2, wup, cw, cb, wdn, fg)
    new_conv = u.reshape(batch, seq_len, d_ff)[:, seq_len - (CONV_W - 1):, :]
    return y, new_conv


def kernel(x_prompt, x_sample, cache_k, cache_v, state_hgrn, state_conv, page_table, c_prompt, c_sample,
           norm1_g, norm2_g, w_ada, b_ada, w_in, hgrn_lb_logits, hg_norm_g, w_a_out, w_b_out, w_o,
           w_up, conv_w, conv_b, w_down, final_g):
    depth = w_in.shape[0]
    assert depth == 1 and hgrn_lb_logits.shape[0] == depth + 1
    bp, tp, d = x_prompt.shape
    bs, ts, _ = x_sample.shape
    d_ff = conv_w.shape[-1]
    w = A_WIDTH

    w_ada_bf = w_ada[0].astype(BF16)
    w_in_bf = w_in[0].astype(BF16)
    wa_bf = w_a_out[0].astype(BF16)
    wb_bf = w_b_out[0].astype(BF16)
    wo_bf = w_o[0].astype(BF16)
    wup_bf = w_up[0].astype(BF16)
    wdn_bf = w_down[0].astype(BF16)
    n1 = norm1_g[0].reshape(1, d)
    n2 = norm2_g[0].reshape(1, d)
    fg = final_g.reshape(1, d)
    hgn = hg_norm_g[0].reshape(1, HG_WIDTH)
    cw = conv_w[0]
    cb = conv_b[0].reshape(1, d_ff)

    mod = _ada_mod(jnp.concatenate([c_prompt, c_sample], axis=0), w_ada_bf, b_ada[0].reshape(1, 6 * d))

    mp = bp * tp
    modp = [mod[:bp, i * d:(i + 1) * d].reshape(bp, 1, d) for i in range(6)]
    xp = x_prompt.reshape(mp, d)
    (q, kt32, vt32, kb, vtb, qh, lf, kh, hi, hg, ga, gb) = _in_proj(
        xp, modp[0], modp[1], n1, w_in_bf, hgrn_lb_logits, per_row=False, seq_len=tp)
    seq3 = lambda a: a.reshape(bp, tp, w)
    oa = _moba_prompt(seq3(q), seq3(kb), vtb).reshape(mp, w)
    ob, sp = _hgrn(seq3(qh), seq3(lf), seq3(kh), seq3(hi), seq3(hg),
                   jnp.zeros((bp, HG_HEADS, HG_DK, HG_DV), F32), hgn)
    x1, h2 = _mix_out(oa, ob.reshape(mp, w), ga, gb, xp, modp[2], modp[3], modp[4], n2, wa_bf, wb_bf, wo_bf,
                      per_row=False, seq_len=tp)
    yp, cp = _ffn_seq(h2, x1, modp[5], jnp.zeros((bp, CONV_W - 1, d_ff), F32), wup_bf, cw, cb, wdn_bf, fg,
                      batch=bp, seq_len=tp)

    ms = bs * ts
    mods_rows = jnp.repeat(mod[bp:], ts, axis=0)
    mods = [mods_rows[:, i * d:(i + 1) * d].reshape(1, ms, d) for i in range(6)]
    xs = x_sample.reshape(ms, d)
    (q_s, k32_s, v32_s, kb_s, vb_s, qh_s, lf_s, kh_s, hi_s, hg_s, ga_s, gb_s) = _in_proj(
        xs, mods[0], mods[1], n1, w_in_bf, hgrn_lb_logits, per_row=True, seq_len=ts)
    seq3s = lambda a: a.reshape(bs, ts, w)
    n_phys = cache_k.shape[1]
    page_t = lambda c: jnp.transpose(c[0], (0, 2, 3, 1)).reshape(n_phys, w, PAGE_SIZE)
    oa_s = _moba_decode(seq3s(q_s), seq3s(kb_s), seq3s(vb_s), page_t(cache_k), page_t(cache_v), page_table)
    t_pad = HG_CHUNK * pl.cdiv(ts, HG_CHUNK)
    padt = lambda a: jnp.pad(seq3s(a), ((0, 0), (0, t_pad - ts), (0, 0)))
    ob_s, ss = _hgrn(padt(qh_s), padt(lf_s), padt(kh_s), padt(hi_s), padt(hg_s), state_hgrn[0], hgn)
    ob_s = ob_s[:, :ts, :].reshape(ms, w)
    x1_s, h2_s = _mix_out(oa_s.reshape(ms, w).astype(BF16), ob_s, ga_s, gb_s, xs, mods[2], mods[3], mods[4], n2,
                          wa_bf, wb_bf, wo_bf, per_row=True, seq_len=ts)
    ys, cs = _ffn_short(h2_s, x1_s, mods[5], state_conv[0], wup_bf, cw, cb, wdn_bf, fg, batch=bs, seq_len=ts)

    kv_p = lambda a: jnp.transpose(a.reshape(1, bp, A_HEADS, A_HD, tp), (0, 1, 4, 2, 3))
    kv_s = lambda a: a.reshape(1, bs, ts, A_HEADS, A_HD)
    return (yp.reshape(bp, tp, d), ys.reshape(bs, ts, d), kv_p(kt32), kv_p(vt32), kv_s(k32_s), kv_s(v32_s),
            sp[None], ss[None], cp[None], cs[None])
```

```python
import functools

import jax
import jax.numpy as jnp
from jax import lax
from jax.experimental import pallas as pl
from jax.experimental.pallas import tpu as pltpu

F32 = jnp.float32
BF16 = jnp.bfloat16

A_HEADS = 8
A_HD = 64
A_WIDTH = A_HEADS * A_HD
MOBA_BLOCK = 256
MOBA_TOPK = 3
GATE_GROUP = 16
HG_HEADS = 4
HG_DK = 128
HG_DV = 128
HG_WIDTH = HG_HEADS * HG_DV
HG_CHUNK = 64
CONV_W = 3
PAGE_SIZE = 128
EPS = 1e-6
NEG_INF = float("-inf")
MASK_BIAS = -1e30
LOG2E = 1.4426950408889634

VMEM_LIMIT_BYTES = 56 * 1024 * 1024
ROW_TILE = 512
FF_TILE = 256
DECODE_PAGES_PER_STEP = 16
SCORE_LOOKAHEAD = 6


def _cparams(semantics, flags=None):
    return pltpu.CompilerParams(dimension_semantics=semantics, vmem_limit_bytes=VMEM_LIMIT_BYTES, flags=flags)


def _nt(a, b):
    return lax.dot_general(a, b, (((1,), (1,)), ((), ())), preferred_element_type=F32)


def _tn(a, b):
    return lax.dot_general(a, b, (((0,), (0,)), ((), ())), preferred_element_type=F32)


def _mm(a, b):
    return jnp.dot(a, b, preferred_element_type=F32)


def _rms(x, g):
    return x * lax.rsqrt(jnp.mean(x * x, axis=-1, keepdims=True) + EPS) * g


def _ada_kernel(c_ref, w_ref, b_ref, o_ref):
    c = c_ref[...]
    s = c * jax.nn.sigmoid(c)
    o_ref[...] = _mm(s.astype(BF16), w_ref[...]) + b_ref[...]


def _ada_mod(c, w_bf, b):
    n, d = c.shape
    cols = w_bf.shape[1]
    tn = 1536
    assert cols % tn == 0
    return pl.pallas_call(
        _ada_kernel,
        grid=(cols // tn,),
        in_specs=[pl.BlockSpec((n, d), lambda j: (0, 0)),
                  pl.BlockSpec((d, tn), lambda j: (0, j)),
                  pl.BlockSpec((1, tn), lambda j: (0, j))],
        out_specs=pl.BlockSpec((n, tn), lambda j: (0, j)),
        out_shape=jax.ShapeDtypeStruct((n, cols), F32),
        compiler_params=_cparams(("arbitrary",)),
        name="ada_mod",
    )(c, w_bf, b)


def _mod_spec(per_row, tm, tiles_per_seq, d):
    if per_row:
        return pl.BlockSpec((None, tm, d), lambda i: (0, i, 0))
    return pl.BlockSpec((None, 1, d), lambda i: (i // tiles_per_seq, 0, 0))


def _inproj_kernel(x_ref, sh_ref, sc_ref, g_ref, w_ref, lbl_ref,
                   q_ref, k32_ref, v32_ref, kb_ref, vb_ref, qh_ref, lf_ref, kh_ref, hi_ref, hg_ref,
                   ga_ref, gb_ref, *, d_model, kv_transposed):
    x = x_ref[...]
    h = _rms(x, g_ref[...]) * (1.0 + sc_ref[...]) + sh_ref[...]
    hb = h.astype(BF16)

    def proj(c0, width):
        return _mm(hb, w_ref[:, c0:c0 + width])

    w = A_WIDTH
    q = proj(0, w)
    q_ref[...] = (q * (A_HD ** -0.5 * LOG2E)).astype(BF16)
    k = proj(w, w)
    kb_ref[...] = k.astype(BF16)
    v = proj(2 * w, w)
    if kv_transposed:
        k32_ref[...] = k.T
        vt = v.T
        v32_ref[...] = vt
        vb_ref[...] = vt.astype(BF16)
    else:
        k32_ref[...] = k
        v32_ref[...] = v
        vb_ref[...] = v.astype(BF16)
    c0 = 3 * w
    hq = proj(c0, w)
    qh_ref[...] = (hq * jax.nn.sigmoid(hq)).astype(BF16)
    hf = proj(c0 + w, w)
    lbl = lbl_ref[...]
    e = jnp.exp(lbl - jnp.max(lbl, axis=0, keepdims=True))
    lb = e[0:1, :] / jnp.sum(e, axis=0, keepdims=True)
    en = jnp.exp(-jnp.abs(hf))
    r = 1.0 / (1.0 + en)
    pos = hf >= 0.0
    sg = jnp.where(pos, r, en * r)
    sgn = jnp.where(pos, en * r, r)
    lf_ref[...] = jnp.log(lb + (1.0 - lb) * sg)
    kh_ref[...] = ((1.0 - lb) * sgn).astype(BF16)
    hi_ref[...] = proj(c0 + 2 * w, w).astype(BF16)
    hg = proj(c0 + 3 * w, w)
    hg_ref[...] = (hg * jax.nn.sigmoid(hg)).astype(BF16)
    c1 = c0 + 4 * w
    ga_ref[...] = jax.nn.sigmoid(proj(c1, d_model)).astype(BF16)
    gb_ref[...] = jax.nn.sigmoid(proj(c1 + d_model, d_model)).astype(BF16)


def _in_proj(x2d, sh1, sc1, norm_g, w_in_bf, lb_logits, *, per_row, seq_len):
    m, d = x2d.shape
    tm = min(ROW_TILE, m)
    assert m % tm == 0 and (per_row or seq_len % tm == 0)
    tps = max(seq_len // tm, 1)
    mod = _mod_spec(per_row, tm, tps, d)
    row = lambda width: pl.BlockSpec((tm, width), lambda i: (i, 0))
    full = lambda a: pl.BlockSpec(a.shape, lambda i: (0,) * a.ndim)
    w = A_WIDTH
    outs = [(w, BF16), (w, F32), (w, F32), (w, BF16), (w, BF16), (w, BF16), (w, F32), (w, BF16), (w, BF16),
            (w, BF16), (d, BF16), (d, BF16)]
    out_specs = [row(width) for width, _ in outs]
    out_shape = [jax.ShapeDtypeStruct((m, width), dt) for width, dt in outs]
    kv_transposed = not per_row
    if kv_transposed:
        for idx in (1, 2, 4):
            out_specs[idx] = pl.BlockSpec((None, w, tm), lambda i: (i // tps, 0, i % tps))
            out_shape[idx] = jax.ShapeDtypeStruct((m // seq_len, w, seq_len), outs[idx][1])
    return pl.pallas_call(
        functools.partial(_inproj_kernel, d_model=d, kv_transposed=kv_transposed),
        grid=(m // tm,),
        in_specs=[row(d), mod, mod, full(norm_g), full(w_in_bf), full(lb_logits)],
        out_specs=out_specs,
        out_shape=out_shape,
        compiler_params=_cparams(("arbitrary",)),
        name="in_proj",
    )(x2d, sh1, sc1, norm_g, w_in_bf, lb_logits)


def _moba_prompt_kernel(q_ref, k_ref, vt_ref, o_ref, km_ref, kmbd_ref, kaug_ref, qaug_ref, m_ref, l_ref, acc_ref,
                        *, nb):
    i = pl.program_id(1)
    blk = MOBA_BLOCK
    gg = GATE_GROUP
    lane = lax.broadcasted_iota(jnp.int32, (blk, 128), 1)

    def head_lanes(x, h):
        part = x[:, (h // 2) * 128:(h // 2 + 1) * 128]
        return pltpu.roll(part, A_HD, axis=1) if h % 2 else part

    @pl.when(i == 0)
    def _():
        km_ref[...] = jnp.zeros_like(km_ref)
        for n in range(nb):
            rs = slice(n * blk, (n + 1) * blk)
            kf = k_ref[rs, :].astype(F32)
            km_ref[n:n + 1, :] = jnp.mean(kf, axis=0, keepdims=True)
            onehot = jnp.where(lane == A_HD + n, 1.0, 0.0)
            for h in range(A_HEADS):
                kaug_ref[rs, h * 128:(h + 1) * 128] = jnp.where(lane < A_HD, head_lanes(kf, h), onehot).astype(BF16)
        km = km_ref[...]
        lane_head = lax.broadcasted_iota(jnp.int32, km.shape, 1) // A_HD
        for h in range(A_HEADS):
            kmbd_ref[h * gg:(h + 1) * gg, :] = jnp.where(lane_head == h, km, 0.0).astype(BF16)

    q = q_ref[...]
    gate = _nt(q, kmbd_ref[...])
    n_id = lane % gg
    rank = jnp.zeros(gate.shape, jnp.int32)
    for d in range(1, gg):
        wrap = n_id + d >= gg
        other = jnp.where(wrap, pltpu.roll(gate, gg - d, axis=1), pltpu.roll(gate, 128 - d, axis=1))
        m_id = jnp.where(wrap, n_id + d - gg, n_id + d)
        beats = (m_id < i) & ((other > gate) | ((other == gate) & (m_id < n_id)))
        rank = rank + beats.astype(jnp.int32)
    bias = jnp.where((n_id < i) & (rank >= MOBA_TOPK), MASK_BIAS, 0.0)
    qf = q.astype(F32)
    for h in range(A_HEADS):
        bias_h = pltpu.roll(bias, (A_HD - h * gg) % 128, axis=1)
        qa = jnp.where(lane < A_HD, head_lanes(qf, h), jnp.where(lane < A_HD + gg, bias_h, 0.0))
        qaug_ref[h] = qa.astype(BF16)

    def scores(j0, h):
        return _nt(kaug_ref[pl.ds(j0, blk), h * 128:(h + 1) * 128], qaug_ref[h])

    own0 = pl.multiple_of(i * blk, blk)
    causal = (lax.broadcasted_iota(jnp.int32, (blk, blk), 0) <= lax.broadcasted_iota(jnp.int32, (blk, blk), 1))

    def own_scores(h):
        return jnp.where(causal, scores(own0, h), NEG_INF)

    def accumulate(j0, s, h, first):
        hr = slice(h * A_HD, (h + 1) * A_HD)
        m_blk = jnp.max(s, axis=0, keepdims=True)
        if first:
            m_new = m_blk
        else:
            m_old = m_ref[h, 0:1, :]
            m_new = jnp.maximum(m_old, m_blk)
            alpha = jnp.exp2(m_old - m_new)
        p = jnp.exp2(s - m_new)
        psum = jnp.sum(p, axis=0, keepdims=True)
        pv = _mm(vt_ref[hr, pl.ds(j0, blk)], p.astype(BF16))
        m_ref[h, 0:1, :] = m_new
        l_ref[h, 0:1, :] = psum if first else alpha * l_ref[h, 0:1, :] + psum
        acc_ref[hr, :] = pv if first else alpha * acc_ref[hr, :] + pv

    def run_blocks(starts, score_fn, first):
        units = [(j0, h) for j0 in starts for h in range(A_HEADS)]
        pending = [score_fn(*u) for u in units[:SCORE_LOOKAHEAD]]
        for n, (j0, h) in enumerate(units):
            if n + SCORE_LOOKAHEAD < len(units):
                pending.append(score_fn(*units[n + SCORE_LOOKAHEAD]))
            accumulate(j0, pending.pop(0), h, first)

    run_blocks([own0], lambda j0, h: own_scores(h), True)

    def pair_body(t, carry):
        j0 = pl.multiple_of(t * (2 * blk), 2 * blk)
        run_blocks([j0, j0 + blk], scores, False)
        return carry

    lax.fori_loop(0, i // 2, pair_body, 0)

    @pl.when(i % 2 == 1)
    def _():
        run_blocks([pl.multiple_of((i - 1) * blk, blk)], scores, False)

    for h in range(A_HEADS):
        hr = slice(h * A_HD, (h + 1) * A_HD)
        acc_ref[hr, :] = acc_ref[hr, :] / l_ref[h, 0:1, :]
    o_ref[...] = acc_ref[...].T.astype(BF16)


def _moba_prompt(q, k, vt):
    b, t, w = q.shape
    assert t % MOBA_BLOCK == 0
    nb = t // MOBA_BLOCK
    assert nb <= GATE_GROUP
    tile = pl.BlockSpec((None, MOBA_BLOCK, w), lambda bi, i: (bi, i, 0))
    return pl.pallas_call(
        functools.partial(_moba_prompt_kernel, nb=nb),
        grid=(b, nb),
        in_specs=[tile, pl.BlockSpec((None, t, w), lambda bi, i: (bi, 0, 0)),
                  pl.BlockSpec((None, w, t), lambda bi, i: (bi, 0, 0))],
        out_specs=tile,
        out_shape=jax.ShapeDtypeStruct((b, t, w), BF16),
        scratch_shapes=[pltpu.VMEM((GATE_GROUP, w), F32), pltpu.VMEM((A_HEADS * GATE_GROUP, w), BF16),
                        pltpu.VMEM((t, A_HEADS * 128), BF16),
                        pltpu.VMEM((A_HEADS, MOBA_BLOCK, 128), BF16),
                        pltpu.VMEM((A_HEADS, 8, MOBA_BLOCK), F32), pltpu.VMEM((A_HEADS, 8, MOBA_BLOCK), F32),
                        pltpu.VMEM((w, MOBA_BLOCK), F32)],
        compiler_params=_cparams(("arbitrary", "arbitrary")),
        name="moba_prompt",
    )(q, k, vt)


def _moba_decode_kernel(pt_ref, qbd_ref, kn_ref, vn_ref, *refs, n_steps, pp, n_blocks, tq):
    k_refs = refs[:pp]
    v_refs = refs[pp:2 * pp]
    o_ref = refs[2 * pp]
    s_ref, g_ref, m_ref, l_ref, acc_ref = refs[2 * pp + 1:]
    del pt_ref
    s_id = pl.program_id(1)
    rows = tq * A_HEADS
    pages_per_block = MOBA_BLOCK // PAGE_SIZE
    qbd = qbd_ref[...]
    lane = lax.broadcasted_iota(jnp.int32, (rows, 128), 1)

    @pl.when(s_id == 0)
    def _():
        g_ref[...] = jnp.zeros_like(g_ref)

    @pl.when(s_id < n_steps)
    def _():
        part = None
        for pi in range(pp):
            page = s_id * pp + pi
            sc = _mm(qbd, k_refs[pi][...].astype(BF16))
            s_ref[:, pl.ds(pl.multiple_of(page * PAGE_SIZE, PAGE_SIZE), PAGE_SIZE)] = sc
            rs = jnp.sum(sc, axis=-1, keepdims=True)
            part = rs if pi % pages_per_block == 0 else part + rs
            if pi % pages_per_block == pages_per_block - 1:
                n = page // pages_per_block
                g_ref[...] = jnp.where(lane == n, part * (1.0 / MOBA_BLOCK), g_ref[...])

    @pl.when(s_id == n_steps - 1)
    def _():
        g = jnp.where(lane < n_blocks, g_ref[...], NEG_INF)
        sel = jnp.zeros(g.shape, F32)
        for _ in range(min(MOBA_TOPK, n_blocks)):
            mx = jnp.max(g, axis=-1, keepdims=True)
            idx = jnp.min(jnp.where(g == mx, lane, 128), axis=-1, keepdims=True)
            pick = lane == idx
            sel = jnp.where(pick, 1.0, sel)
            g = jnp.where(pick, NEG_INF, g)

        def mask_block(n, m):
            picked = jnp.max(jnp.where(lane == n, sel, 0.0), axis=-1, keepdims=True)
            b0 = pl.multiple_of(n * MOBA_BLOCK, MOBA_BLOCK)
            sb = jnp.where(picked > 0.0, s_ref[:, pl.ds(b0, MOBA_BLOCK)], NEG_INF)
            s_ref[:, pl.ds(b0, MOBA_BLOCK)] = sb
            return jnp.maximum(m, jnp.max(sb, axis=-1, keepdims=True))

        m_past = lax.fori_loop(0, n_blocks, mask_block, jnp.full((rows, 1), NEG_INF, F32),
                               unroll=next(u for u in (8, 4, 2, 1) if n_blocks % u == 0))
        sn = _nt(qbd, kn_ref[...])
        qi = lax.broadcasted_iota(jnp.int32, sn.shape, 0) // A_HEADS
        sn = jnp.where(lane <= qi, sn, NEG_INF)
        m = jnp.maximum(m_past, jnp.max(sn, axis=-1, keepdims=True))
        pn = jnp.exp2(sn - m)
        m_ref[...] = jnp.broadcast_to(m, m_ref.shape)
        l_ref[...] = jnp.broadcast_to(jnp.sum(pn, axis=-1, keepdims=True), l_ref.shape)
        acc_ref[...] = _mm(pn.astype(BF16), vn_ref[...])

    @pl.when(s_id >= n_steps)
    def _():
        m = m_ref[:, 0:1]
        lsum = jnp.zeros((rows, 1), F32)
        acc = jnp.zeros(acc_ref.shape, F32)
        for pi in range(pp):
            page = (s_id - n_steps) * pp + pi
            sc = s_ref[:, pl.ds(pl.multiple_of(page * PAGE_SIZE, PAGE_SIZE), PAGE_SIZE)]
            p = jnp.exp2(sc - m)
            lsum = lsum + jnp.sum(p, axis=-1, keepdims=True)
            acc = acc + _nt(p.astype(BF16), v_refs[pi][...].astype(BF16))
        l_ref[...] = l_ref[...] + lsum
        acc_ref[...] = acc_ref[...] + acc

    @pl.when(s_id == 2 * n_steps - 1)
    def _():
        o = acc_ref[...] / l_ref[:, 0:1]
        r_head = lax.broadcasted_iota(jnp.int32, o.shape, 0) % A_HEADS
        l_head = lax.broadcasted_iota(jnp.int32, o.shape, 1) // A_HD
        o = jnp.where(r_head == l_head, o, 0.0)
        o_ref[...] = jnp.zeros_like(o_ref)
        o_ref[0:tq, :] = jnp.sum(o.reshape(tq, A_HEADS, o.shape[-1]), axis=1)


def _moba_decode(q, k_new, v_new, cache_kt, cache_vt, page_table):
    b, tq, w = q.shape
    n_pages = page_table.shape[1]
    past = n_pages * PAGE_SIZE
    assert past % MOBA_BLOCK == 0 and tq <= PAGE_SIZE and tq <= MOBA_BLOCK
    n_blocks = past // MOBA_BLOCK
    assert n_blocks <= 128
    pp = DECODE_PAGES_PER_STEP
    assert n_pages % pp == 0 and pp % (MOBA_BLOCK // PAGE_SIZE) == 0
    n_steps = n_pages // pp
    rows = tq * A_HEADS
    head_of_lane = jnp.arange(w, dtype=jnp.int32) // A_HD
    head_mask = (head_of_lane[None, :] == jnp.arange(A_HEADS, dtype=jnp.int32)[:, None])
    qbd = jnp.where(head_mask[None, None], q[:, :, None, :], jnp.zeros((), q.dtype)).reshape(b, rows, w)
    pad = ((0, 0), (0, PAGE_SIZE - tq), (0, 0))
    kn = jnp.pad(k_new, pad)
    vn = jnp.pad(v_new, pad)

    def k_map(pi):
        return lambda bi, s, pt: (pt[bi, jnp.minimum(s, n_steps - 1) * pp + pi], 0, 0)

    def v_map(pi):
        return lambda bi, s, pt: (pt[bi, jnp.maximum(s - n_steps, 0) * pp + pi], 0, 0)

    per_b = lambda r: pl.BlockSpec((None, r, w), lambda bi, s, pt: (bi, 0, 0))
    page = lambda imap: pl.BlockSpec((None, w, PAGE_SIZE), imap)
    out_rows = 8 * pl.cdiv(tq, 8)
    grid_spec = pltpu.PrefetchScalarGridSpec(
        num_scalar_prefetch=1,
        grid=(b, 2 * n_steps),
        in_specs=[per_b(rows), per_b(PAGE_SIZE), per_b(PAGE_SIZE)]
        + [page(k_map(pi)) for pi in range(pp)] + [page(v_map(pi)) for pi in range(pp)],
        out_specs=per_b(out_rows),
        scratch_shapes=[pltpu.VMEM((rows, past), F32), pltpu.VMEM((rows, 128), F32),
                        pltpu.VMEM((rows, 128), F32), pltpu.VMEM((rows, 128), F32),
                        pltpu.VMEM((rows, w), F32)],
    )
    out = pl.pallas_call(
        functools.partial(_moba_decode_kernel, n_steps=n_steps, pp=pp, n_blocks=n_blocks, tq=tq),
        grid_spec=grid_spec,
        out_shape=jax.ShapeDtypeStruct((b, out_rows, w), F32),
        compiler_params=_cparams(("arbitrary", "arbitrary")),
        name="moba_decode",
    )(page_table, qbd, kn, vn, *([cache_kt] * pp), *([cache_vt] * pp))
    return out[:, :tq, :]


def _hgrn_kernel(qh_ref, lf_ref, kh_ref, v_ref, hg_ref, s0_ref, g_ref, o_ref, st_out_ref, st_ref, *, n_chunks):
    t = pl.program_id(1)
    c = HG_CHUNK

    @pl.when(t == 0)
    def _():
        for h in range(HG_HEADS):
            st_ref[h] = s0_ref[h].T

    row = lax.broadcasted_iota(jnp.int32, (c, HG_DK), 0)
    tril = lax.broadcasted_iota(jnp.int32, (c, c), 1) <= lax.broadcasted_iota(jnp.int32, (c, c), 0)
    for ci in range(n_chunks):
        rs = slice(ci * c, (ci + 1) * c)
        for h in range(HG_HEADS):
            ks = slice(h * HG_DK, (h + 1) * HG_DK)
            vs = slice(h * HG_DV, (h + 1) * HG_DV)
            b = lf_ref[rs, ks]
            sh = 1
            while sh < c:
                b = b + jnp.where(row >= sh, pltpu.roll(b, sh, axis=0), 0.0)
                sh *= 2
            ref = b[c // 2 - 1:c // 2, :]
            last = b[c - 1:c, :]
            qh = qh_ref[rs, ks].astype(F32)
            kh = kh_ref[rs, ks].astype(F32)
            v = v_ref[rs, vs]
            qe = (qh * jnp.exp(b - ref)).astype(BF16)
            ke = (kh * jnp.exp(ref - b)).astype(BF16)
            a = jnp.where(tril, _nt(qe, ke), 0.0)
            qs = (qh * jnp.exp(b)).astype(BF16)
            kd = (kh * jnp.exp(last - b)).astype(BF16)
            st = st_ref[h]
            o = _mm(a.astype(BF16), v) + _nt(qs, st.astype(BF16))
            st_ref[h] = st * jnp.exp(last) + _tn(v, kd)
            y = _rms(o, g_ref[:, vs]) * hg_ref[rs, vs].astype(F32)
            o_ref[rs, vs] = y.astype(BF16)

    @pl.when(t == pl.num_programs(1) - 1)
    def _():
        for h in range(HG_HEADS):
            st_out_ref[h] = st_ref[h].T


def _hgrn(qh, lf, kh, v, hg, s0, norm_g):
    b, t, w = qh.shape
    assert t % HG_CHUNK == 0
    rows = ROW_TILE if t % ROW_TILE == 0 else HG_CHUNK
    seq = pl.BlockSpec((None, rows, w), lambda bi, ti: (bi, ti, 0))
    state = pl.BlockSpec((None, HG_HEADS, HG_DK, HG_DV), lambda bi, ti: (bi, 0, 0, 0))
    return pl.pallas_call(
        functools.partial(_hgrn_kernel, n_chunks=rows // HG_CHUNK),
        grid=(b, t // rows),
        in_specs=[seq, seq, seq, seq, seq, state, pl.BlockSpec(norm_g.shape, lambda bi, ti: (0, 0))],
        out_specs=[seq, state],
        out_shape=[jax.ShapeDtypeStruct((b, t, w), BF16), jax.ShapeDtypeStruct(s0.shape, F32)],
        scratch_shapes=[pltpu.VMEM((HG_HEADS, HG_DV, HG_DK), F32)],
        compiler_params=_cparams(("arbitrary", "arbitrary")),
        name="hgrn",
    )(qh, lf, kh, v, hg, s0, norm_g)


def _mix_kernel(oa_ref, ob_ref, ga_ref, gb_ref, x_ref, g1_ref, sh2_ref, sc2_ref, n2_ref, wa_ref, wb_ref, wo_ref,
                x1_ref, h2_ref):
    ya = _mm(oa_ref[...], wa_ref[...])
    yb = _mm(ob_ref[...], wb_ref[...])
    mixed = ga_ref[...].astype(F32) * ya + gb_ref[...].astype(F32) * yb
    x1 = x_ref[...] + g1_ref[...] * _mm(mixed.astype(BF16), wo_ref[...])
    x1_ref[...] = x1
    h2_ref[...] = (_rms(x1, n2_ref[...]) * (1.0 + sc2_ref[...]) + sh2_ref[...]).astype(BF16)


def _mix_out(oa, ob, ga, gb, x2d, g1, sh2, sc2, norm2_g, wa, wb, wo, *, per_row, seq_len):
    m, d = x2d.shape
    tm = min(ROW_TILE, m)
    tps = max(seq_len // tm, 1)
    mod = _mod_spec(per_row, tm, tps, d)
    row = lambda width: pl.BlockSpec((tm, width), lambda i: (i, 0))
    full = lambda a: pl.BlockSpec(a.shape, lambda i: (0,) * a.ndim)
    return pl.pallas_call(
        _mix_kernel,
        grid=(m // tm,),
        in_specs=[row(A_WIDTH), row(HG_WIDTH), row(d), row(d), row(d), mod, mod, mod, full(norm2_g),
                  full(wa), full(wb), full(wo)],
        out_specs=[row(d), row(d)],
        out_shape=[jax.ShapeDtypeStruct((m, d), F32), jax.ShapeDtypeStruct((m, d), BF16)],
        compiler_params=_cparams(("arbitrary",)),
        name="mix_out",
    )(oa, ob, ga, gb, x2d, g1, sh2, sc2, norm2_g, wa, wb, wo)


def _gelu(x):
    return 0.5 * x * (1.0 + lax.erf(x * (2.0 ** -0.5)))


def _ffn_body(h2_ref, x1_ref, g2_ref, wup_ref, cw_ref, cb_ref, wdn_ref, fg_ref, y_ref, acc_ref, prev_rows, emit_u):
    tm, d_ff = h2_ref.shape[0], cw_ref.shape[1]
    h2 = h2_ref[...]
    acc_ref[...] = jnp.zeros_like(acc_ref)
    n_ft = d_ff // FF_TILE

    def up_proj(ft):
        return (_mm(h2, wup_ref[:, ft * FF_TILE:(ft + 1) * FF_TILE]),
                _mm(h2, wup_ref[:, d_ff + ft * FF_TILE:d_ff + (ft + 1) * FF_TILE]))

    pending = [up_proj(0)]
    for ft in range(n_ft):
        fs = slice(ft * FF_TILE, (ft + 1) * FF_TILE)
        if ft + 1 < n_ft:
            pending.append(up_proj(ft + 1))
        u, v = pending.pop(0)
        um1, um2 = prev_rows(u, fs)
        uc = cb_ref[:, fs] + cw_ref[0:1, fs] * um2 + cw_ref[1:2, fs] * um1 + cw_ref[2:3, fs] * u
        act = (_gelu(uc) * v).astype(BF16)
        acc_ref[...] += _mm(act, wdn_ref[fs, :])
        emit_u(u, fs)
    x2 = x1_ref[...] + g2_ref[...] * acc_ref[...]
    y_ref[...] = _rms(x2, fg_ref[...])


def _ffn_seq_kernel(h2_ref, x1_ref, g2_ref, cp_ref, wup_ref, cw_ref, cb_ref, wdn_ref, fg_ref,
                    y_ref, nc_ref, acc_ref, carry_ref):
    tm = h2_ref.shape[0]

    @pl.when(pl.program_id(1) == 0)
    def _():
        carry_ref[...] = cp_ref[...]

    row = lax.broadcasted_iota(jnp.int32, (tm, FF_TILE), 0)

    def prev_rows(u, fs):
        c0 = carry_ref[0:1, fs]
        c1 = carry_ref[1:2, fs]
        um1 = jnp.where(row == 0, c1, pltpu.roll(u, 1, axis=0))
        um2 = jnp.where(row == 0, c0, jnp.where(row == 1, c1, pltpu.roll(u, 2, axis=0)))
        return um1, um2

    def emit_u(u, fs):
        tail = u[tm - (CONV_W - 1):tm, :]
        carry_ref[:, fs] = tail
        nc_ref[:, fs] = tail

    _ffn_body(h2_ref, x1_ref, g2_ref, wup_ref, cw_ref, cb_ref, wdn_ref, fg_ref, y_ref, acc_ref, prev_rows, emit_u)


def _ffn_short_kernel(h2_ref, x1_ref, g2_ref, p1_ref, p2_ref, wup_ref, cw_ref, cb_ref, wdn_ref, fg_ref,
                      y_ref, u_ref, acc_ref, *, seq_len):
    tm = h2_ref.shape[0]
    tpos = lax.broadcasted_iota(jnp.int32, (tm, FF_TILE), 0) % seq_len

    def prev_rows(u, fs):
        um1 = jnp.where(tpos >= 1, pltpu.roll(u, 1, axis=0), p1_ref[:, fs])
        um2 = jnp.where(tpos >= 2, pltpu.roll(u, 2, axis=0), p2_ref[:, fs])
        return um1, um2

    def emit_u(u, fs):
        u_ref[:, fs] = u

    _ffn_body(h2_ref, x1_ref, g2_ref, wup_ref, cw_ref, cb_ref, wdn_ref, fg_ref, y_ref, acc_ref, prev_rows, emit_u)


def _ffn_seq(h2, x1, g2, conv_prev, wup, cw, cb, wdn, fg, *, batch, seq_len):
    m, d = x1.shape
    d_ff = cw.shape[1]
    tm = min(ROW_TILE, seq_len)
    assert seq_len % tm == 0 and d_ff % FF_TILE == 0 and tm >= CONV_W - 1
    tps = seq_len // tm
    row = lambda width: pl.BlockSpec((tm, width), lambda bi, ti: (bi * tps + ti, 0))
    full = lambda a: pl.BlockSpec(a.shape, lambda bi, ti: (0,) * a.ndim)
    per_b = lambda r, width: pl.BlockSpec((None, r, width), lambda bi, ti: (bi, 0, 0))
    return pl.pallas_call(
        _ffn_seq_kernel,
        grid=(batch, tps),
        in_specs=[row(d), row(d), per_b(1, d), per_b(CONV_W - 1, d_ff), full(wup), full(cw), full(cb), full(wdn),
                  full(fg)],
        out_specs=[row(d), per_b(CONV_W - 1, d_ff)],
        out_shape=[jax.ShapeDtypeStruct((m, d), F32), jax.ShapeDtypeStruct((batch, CONV_W - 1, d_ff), F32)],
        scratch_shapes=[pltpu.VMEM((tm, d), F32), pltpu.VMEM((CONV_W - 1, d_ff), F32)],
        compiler_params=_cparams(("arbitrary", "arbitrary")),
        name="conv_ffn",
    )(h2, x1, g2, conv_prev, wup, cw, cb, wdn, fg)


def _ffn_short(h2, x1, g2_rows, conv_prev, wup, cw, cb, wdn, fg, *, batch, seq_len):
    m, d = x1.shape
    d_ff = cw.shape[1]
    assert CONV_W == 3 and seq_len >= CONV_W - 1
    zeros = lambda n: jnp.zeros((batch, n, d_ff), F32)
    p1 = jnp.concatenate([conv_prev[:, 1:2], zeros(seq_len - 1)], axis=1).reshape(m, d_ff)
    p2 = jnp.concatenate([conv_prev[:, 0:2], zeros(seq_len - 2)], axis=1).reshape(m, d_ff)
    tm = m
    row = lambda width: pl.BlockSpec((tm, width), lambda i: (i, 0))
    full = lambda a: pl.BlockSpec(a.shape, lambda i: (0,) * a.ndim)
    y, u = pl.pallas_call(
        functools.partial(_ffn_short_kernel, seq_len=seq_len),
        grid=(m // tm,),
        in_specs=[row(d), row(d), pl.BlockSpec((None, tm, d), lambda i: (0, i, 0)), row(d_ff), row(d_ff),
                  full(wup), full(cw), full(cb), full(wdn), full(fg)],
        out_specs=[row(d), row(d_ff)],
        out_shape=[jax.ShapeDtypeStruct((m, d), F32), jax.ShapeDtypeStruct((m, d_ff), F32)],
        scratch_shapes=[pltpu.VMEM((tm, d), F32)],
        compiler_params=_cparams(("arbitrary",)),
        name="conv_ffn_short",
    )(h2, x1, g2_rows, p1, p2, wup, cw, cb, wdn, fg)
    new_conv = u.reshape(batch, seq_len, d_ff)[:, seq_len - (CONV_W - 1):, :]
    return y, new_conv


def kernel(x_prompt, x_sample, cache_k, cache_v, state_hgrn, state_conv, page_table, c_prompt, c_sample,
           norm1_g, norm2_g, w_ada, b_ada, w_in, hgrn_lb_logits, hg_norm_g, w_a_out, w_b_out, w_o,
           w_up, conv_w, conv_b, w_down, final_g):
    depth = w_in.shape[0]
    assert depth == 1 and hgrn_lb_logits.shape[0] == depth + 1
    bp, tp, d = x_prompt.shape
    bs, ts, _ = x_sample.shape
    d_ff = conv_w.shape[-1]
    w = A_WIDTH

    w_ada_bf = w_ada[0].astype(BF16)
    w_in_bf = w_in[0].astype(BF16)
    wa_bf = w_a_out[0].astype(BF16)
    wb_bf = w_b_out[0].astype(BF16)
    wo_bf = w_o[0].astype(BF16)
    wup_bf = w_up[0].astype(BF16)
    wdn_bf = w_down[0].astype(BF16)
    n1 = norm1_g[0].reshape(1, d)
    n2 = norm2_g[0].reshape(1, d)
    fg = final_g.reshape(1, d)
    hgn = hg_norm_g[0].reshape(1, HG_WIDTH)
    cw = conv_w[0]
    cb = conv_b[0].reshape(1, d_ff)

    mod = _ada_mod(jnp.concatenate([c_prompt, c_sample], axis=0), w_ada_bf, b_ada[0].reshape(1, 6 * d))

    mp = bp * tp
    modp = [mod[:bp, i * d:(i + 1) * d].reshape(bp, 1, d) for i in range(6)]
    xp = x_prompt.reshape(mp, d)
    (q, kt32, vt32, kb, vtb, qh, lf, kh, hi, hg, ga, gb) = _in_proj(
        xp, modp[0], modp[1], n1, w_in_bf, hgrn_lb_logits, per_row=False, seq_len=tp)
    seq3 = lambda a: a.reshape(bp, tp, w)
    oa = _moba_prompt(seq3(q), seq3(kb), vtb).reshape(mp, w)
    ob, sp = _hgrn(seq3(qh), seq3(lf), seq3(kh), seq3(hi), seq3(hg),
                   jnp.zeros((bp, HG_HEADS, HG_DK, HG_DV), F32), hgn)
    x1, h2 = _mix_out(oa, ob.reshape(mp, w), ga, gb, xp, modp[2], modp[3], modp[4], n2, wa_bf, wb_bf, wo_bf,
                      per_row=False, seq_len=tp)
    yp, cp = _ffn_seq(h2, x1, modp[5], jnp.zeros((bp, CONV_W - 1, d_ff), F32), wup_bf, cw, cb, wdn_bf, fg,
                      batch=bp, seq_len=tp)

    ms = bs * ts
    mods_rows = jnp.repeat(mod[bp:], ts, axis=0)
    mods = [mods_rows[:, i * d:(i + 1) * d].reshape(1, ms, d) for i in range(6)]
    xs = x_sample.reshape(ms, d)
    (q_s, k32_s, v32_s, kb_s, vb_s, qh_s, lf_s, kh_s, hi_s, hg_s, ga_s, gb_s) = _in_proj(
        xs, mods[0], mods[1], n1, w_in_bf, hgrn_lb_logits, per_row=True, seq_len=ts)
    seq3s = lambda a: a.reshape(bs, ts, w)
    n_phys = cache_k.shape[1]
    page_t = lambda c: jnp.transpose(c[0], (0, 2, 3, 1)).reshape(n_phys, w, PAGE_SIZE)
    oa_s = _moba_decode(seq3s(q_s), seq3s(kb_s), seq3s(vb_s), page_t(cache_k), page_t(cache_v), page_table)
    t_pad = HG_CHUNK * pl.cdiv(ts, HG_CHUNK)
    padt = lambda a: jnp.pad(seq3s(a), ((0, 0), (0, t_pad - ts), (0, 0)))
    ob_s, ss = _hgrn(padt(qh_s), padt(lf_s), padt(kh_s), padt(hi_s), padt(hg_s), state_hgrn[0], hgn)
    ob_s = ob_s[:, :ts, :].reshape(ms, w)
    x1_s, h2_s = _mix_out(oa_s.reshape(ms, w).astype(BF16), ob_s, ga_s, gb_s, xs, mods[2], mods[3], mods[4], n2,
                          wa_bf, wb_bf, wo_bf, per_row=True, seq_len=ts)
    ys, cs = _ffn_short(h2_s, x1_s, mods[5], state_conv[0], wup_bf, cw, cb, wdn_bf, fg, batch=bs, seq_len=ts)

    kv_p = lambda a: jnp.transpose(a.reshape(1, bp, A_HEADS, A_HD, tp), (0, 1, 4, 2, 3))
    kv_s = lambda a: a.reshape(1, bs, ts, A_HEADS, A_HD)
    return (yp.reshape(bp, tp, d), ys.reshape(bs, ts, d), kv_p(kt32), kv_p(vt32), kv_s(k32_s), kv_s(v32_s),
            sp[None], ss[None], cp[None], cs[None])
```

```python
import functools

import jax
import jax.numpy as jnp
from jax import lax
from jax.experimental import pallas as pl
from jax.experimental.pallas import tpu as pltpu

F32 = jnp.float32
BF16 = jnp.bfloat16

A_HEADS = 8
A_HD = 64
A_WIDTH = A_HEADS * A_HD
MOBA_BLOCK = 256
MOBA_TOPK = 3
GATE_GROUP = 16
HG_HEADS = 4
HG_DK = 128
HG_DV = 128
HG_WIDTH = HG_HEADS * HG_DV
HG_CHUNK = 64
CONV_W = 3
PAGE_SIZE = 128
EPS = 1e-6
NEG_INF = float("-inf")
MASK_BIAS = -1e30
LOG2E = 1.4426950408889634

VMEM_LIMIT_BYTES = 56 * 1024 * 1024
ROW_TILE = 512
FF_TILE = 256
DECODE_PAGES_PER_STEP = 32
SCORE_LOOKAHEAD = 6


def _cparams(semantics, flags=None):
    return pltpu.CompilerParams(dimension_semantics=semantics, vmem_limit_bytes=VMEM_LIMIT_BYTES, flags=flags)


def _nt(a, b):
    return lax.dot_general(a, b, (((1,), (1,)), ((), ())), preferred_element_type=F32)


def _tn(a, b):
    return lax.dot_general(a, b, (((0,), (0,)), ((), ())), preferred_element_type=F32)


def _mm(a, b):
    return jnp.dot(a, b, preferred_element_type=F32)


def _rms(x, g):
    return x * lax.rsqrt(jnp.mean(x * x, axis=-1, keepdims=True) + EPS) * g


def _ada_kernel(c_ref, w_ref, b_ref, o_ref):
    c = c_ref[...]
    s = c * jax.nn.sigmoid(c)
    o_ref[...] = _mm(s.astype(BF16), w_ref[...]) + b_ref[...]


def _ada_mod(c, w_bf, b):
    n, d = c.shape
    cols = w_bf.shape[1]
    tn = 1536
    assert cols % tn == 0
    return pl.pallas_call(
        _ada_kernel,
        grid=(cols // tn,),
        in_specs=[pl.BlockSpec((n, d), lambda j: (0, 0)),
                  pl.BlockSpec((d, tn), lambda j: (0, j)),
                  pl.BlockSpec((1, tn), lambda j: (0, j))],
        out_specs=pl.BlockSpec((n, tn), lambda j: (0, j)),
        out_shape=jax.ShapeDtypeStruct((n, cols), F32),
        compiler_params=_cparams(("arbitrary",)),
        name="ada_mod",
    )(c, w_bf, b)


def _mod_spec(per_row, tm, tiles_per_seq, d):
    if per_row:
        return pl.BlockSpec((None, tm, d), lambda i: (0, i, 0))
    return pl.BlockSpec((None, 1, d), lambda i: (i // tiles_per_seq, 0, 0))


def _inproj_kernel(x_ref, sh_ref, sc_ref, g_ref, w_ref, lbl_ref,
                   q_ref, k32_ref, v32_ref, kb_ref, vb_ref, qh_ref, lf_ref, kh_ref, hi_ref, hg_ref,
                   ga_ref, gb_ref, *, d_model, kv_transposed):
    x = x_ref[...]
    h = _rms(x, g_ref[...]) * (1.0 + sc_ref[...]) + sh_ref[...]
    hb = h.astype(BF16)

    def proj(c0, width):
        return _mm(hb, w_ref[:, c0:c0 + width])

    w = A_WIDTH
    q = proj(0, w) * (A_HD ** -0.5 * LOG2E)
    k = proj(w, w)
    kb_ref[...] = k.astype(BF16)
    v = proj(2 * w, w)
    if kv_transposed:
        q_ref[...] = q.T.astype(BF16)
        k32_ref[...] = k.T
        vt = v.T
        v32_ref[...] = vt
        vb_ref[...] = vt.astype(BF16)
    else:
        q_ref[...] = q.astype(BF16)
        k32_ref[...] = k
        v32_ref[...] = v
        vb_ref[...] = v.astype(BF16)
    c0 = 3 * w
    hq = proj(c0, w)
    qh_ref[...] = (hq * jax.nn.sigmoid(hq)).astype(BF16)
    hf = proj(c0 + w, w)
    lbl = lbl_ref[...]
    e = jnp.exp(lbl - jnp.max(lbl, axis=0, keepdims=True))
    lb = e[0:1, :] / jnp.sum(e, axis=0, keepdims=True)
    en = jnp.exp(-jnp.abs(hf))
    r = 1.0 / (1.0 + en)
    pos = hf >= 0.0
    sg = jnp.where(pos, r, en * r)
    sgn = jnp.where(pos, en * r, r)
    lf_ref[...] = jnp.log(lb + (1.0 - lb) * sg)
    kh_ref[...] = ((1.0 - lb) * sgn).astype(BF16)
    hi_ref[...] = proj(c0 + 2 * w, w).astype(BF16)
    hg = proj(c0 + 3 * w, w)
    hg_ref[...] = (hg * jax.nn.sigmoid(hg)).astype(BF16)
    c1 = c0 + 4 * w
    ga_ref[...] = jax.nn.sigmoid(proj(c1, d_model)).astype(BF16)
    gb_ref[...] = jax.nn.sigmoid(proj(c1 + d_model, d_model)).astype(BF16)


def _in_proj(x2d, sh1, sc1, norm_g, w_in_bf, lb_logits, *, per_row, seq_len):
    m, d = x2d.shape
    tm = min(ROW_TILE, m)
    assert m % tm == 0 and (per_row or seq_len % tm == 0)
    tps = max(seq_len // tm, 1)
    mod = _mod_spec(per_row, tm, tps, d)
    row = lambda width: pl.BlockSpec((tm, width), lambda i: (i, 0))
    full = lambda a: pl.BlockSpec(a.shape, lambda i: (0,) * a.ndim)
    w = A_WIDTH
    outs = [(w, BF16), (w, F32), (w, F32), (w, BF16), (w, BF16), (w, BF16), (w, F32), (w, BF16), (w, BF16),
            (w, BF16), (d, BF16), (d, BF16)]
    out_specs = [row(width) for width, _ in outs]
    out_shape = [jax.ShapeDtypeStruct((m, width), dt) for width, dt in outs]
    kv_transposed = not per_row
    if kv_transposed:
        for idx in (0, 1, 2, 4):
            out_specs[idx] = pl.BlockSpec((None, w, tm), lambda i: (i // tps, 0, i % tps))
            out_shape[idx] = jax.ShapeDtypeStruct((m // seq_len, w, seq_len), outs[idx][1])
    return pl.pallas_call(
        functools.partial(_inproj_kernel, d_model=d, kv_transposed=kv_transposed),
        grid=(m // tm,),
        in_specs=[row(d), mod, mod, full(norm_g), full(w_in_bf), full(lb_logits)],
        out_specs=out_specs,
        out_shape=out_shape,
        compiler_params=_cparams(("arbitrary",)),
        name="in_proj",
    )(x2d, sh1, sc1, norm_g, w_in_bf, lb_logits)


def _moba_prompt_kernel(qt_ref, k_ref, vt_ref, o_ref, km_ref, kmbd_ref, kaug_ref, qaug_ref, gate_ref, m_ref, l_ref,
                        acc_ref, *, nb):
    i = pl.program_id(1)
    blk = MOBA_BLOCK
    gg = GATE_GROUP
    lane = lax.broadcasted_iota(jnp.int32, (blk, 128), 1)

    def head_lanes(x, h):
        part = x[:, (h // 2) * 128:(h // 2 + 1) * 128]
        return pltpu.roll(part, A_HD, axis=1) if h % 2 else part

    @pl.when(i == 0)
    def _():
        qaug_ref[...] = jnp.zeros_like(qaug_ref)
        km_ref[...] = jnp.zeros_like(km_ref)
        for n in range(nb):
            rs = slice(n * blk, (n + 1) * blk)
            kf = k_ref[rs, :].astype(F32)
            km_ref[n:n + 1, :] = jnp.mean(kf, axis=0, keepdims=True)
            onehot = jnp.where(lane == A_HD + n, 1.0, 0.0)
            for h in range(A_HEADS):
                kaug_ref[rs, h * 128:(h + 1) * 128] = jnp.where(lane < A_HD, head_lanes(kf, h), onehot).astype(BF16)
        km = km_ref[...]
        lane_head = lax.broadcasted_iota(jnp.int32, km.shape, 1) // A_HD
        for h in range(A_HEADS):
            kmbd_ref[h * gg:(h + 1) * gg, :] = jnp.where(lane_head == h, km, 0.0).astype(BF16)

    qt = qt_ref[...]
    gate_ref[...] = _mm(kmbd_ref[...], qt)
    n_id = lax.broadcasted_iota(jnp.int32, (gg, blk), 0)
    for h in range(A_HEADS):
        g_h = gate_ref[h * gg:(h + 1) * gg, :]
        rank = jnp.zeros(g_h.shape, jnp.int32)
        for m in range(nb):
            g_m = gate_ref[h * gg + m:h * gg + m + 1, :]
            beats = (g_m > g_h) | ((g_m == g_h) & (m < n_id))
            rank = rank + jnp.where(beats, jnp.where(m < i, 1, 0), 0)
        bias_h = jnp.where((n_id < i) & (rank >= MOBA_TOPK), MASK_BIAS, 0.0)
        qaug_ref[h, 0:A_HD, :] = qt[h * A_HD:(h + 1) * A_HD, :]
        qaug_ref[h, A_HD:A_HD + gg, :] = bias_h.astype(BF16)

    def scores(j0, h):
        return _mm(kaug_ref[pl.ds(j0, blk), h * 128:(h + 1) * 128], qaug_ref[h])

    own0 = pl.multiple_of(i * blk, blk)
    causal = (lax.broadcasted_iota(jnp.int32, (blk, blk), 0) <= lax.broadcasted_iota(jnp.int32, (blk, blk), 1))

    def own_scores(h):
        return jnp.where(causal, scores(own0, h), NEG_INF)

    def accumulate(j0, s, h, first):
        hr = slice(h * A_HD, (h + 1) * A_HD)
        m_blk = jnp.max(s, axis=0, keepdims=True)
        if first:
            m_new = m_blk
        else:
            m_old = m_ref[h, 0:1, :]
            m_new = jnp.maximum(m_old, m_blk)
            alpha = jnp.exp2(m_old - m_new)
        p = jnp.exp2(s - m_new)
        psum = jnp.sum(p, axis=0, keepdims=True)
        pv = _mm(vt_ref[hr, pl.ds(j0, blk)], p.astype(BF16))
        m_ref[h, 0:1, :] = m_new
        l_ref[h, 0:1, :] = psum if first else alpha * l_ref[h, 0:1, :] + psum
        acc_ref[hr, :] = pv if first else alpha * acc_ref[hr, :] + pv

    def run_blocks(starts, score_fn, first):
        units = [(j0, h) for j0 in starts for h in range(A_HEADS)]
        pending = [score_fn(*u) for u in units[:SCORE_LOOKAHEAD]]
        for n, (j0, h) in enumerate(units):
            if n + SCORE_LOOKAHEAD < len(units):
                pending.append(score_fn(*units[n + SCORE_LOOKAHEAD]))
            accumulate(j0, pending.pop(0), h, first)

    run_blocks([own0], lambda j0, h: own_scores(h), True)

    def pair_body(t, carry):
        j0 = pl.multiple_of(t * (2 * blk), 2 * blk)
        run_blocks([j0, j0 + blk], scores, False)
        return carry

    lax.fori_loop(0, i // 2, pair_body, 0)

    @pl.when(i % 2 == 1)
    def _():
        run_blocks([pl.multiple_of((i - 1) * blk, blk)], scores, False)

    for h in range(A_HEADS):
        hr = slice(h * A_HD, (h + 1) * A_HD)
        acc_ref[hr, :] = acc_ref[hr, :] / l_ref[h, 0:1, :]
    o_ref[...] = acc_ref[...].T.astype(BF16)


def _moba_prompt(qt, k, vt):
    b, t, w = k.shape
    assert t % MOBA_BLOCK == 0
    nb = t // MOBA_BLOCK
    assert nb <= GATE_GROUP
    return pl.pallas_call(
        functools.partial(_moba_prompt_kernel, nb=nb),
        grid=(b, nb),
        in_specs=[pl.BlockSpec((None, w, MOBA_BLOCK), lambda bi, i: (bi, 0, i)),
                  pl.BlockSpec((None, t, w), lambda bi, i: (bi, 0, 0)),
                  pl.BlockSpec((None, w, t), lambda bi, i: (bi, 0, 0))],
        out_specs=pl.BlockSpec((None, MOBA_BLOCK, w), lambda bi, i: (bi, i, 0)),
        out_shape=jax.ShapeDtypeStruct((b, t, w), BF16),
        scratch_shapes=[pltpu.VMEM((GATE_GROUP, w), F32), pltpu.VMEM((A_HEADS * GATE_GROUP, w), BF16),
                        pltpu.VMEM((t, A_HEADS * 128), BF16),
                        pltpu.VMEM((A_HEADS, 128, MOBA_BLOCK), BF16),
                        pltpu.VMEM((A_HEADS * GATE_GROUP, MOBA_BLOCK), F32),
                        pltpu.VMEM((A_HEADS, 8, MOBA_BLOCK), F32), pltpu.VMEM((A_HEADS, 8, MOBA_BLOCK), F32),
                        pltpu.VMEM((w, MOBA_BLOCK), F32)],
        compiler_params=_cparams(("arbitrary", "arbitrary")),
        name="moba_prompt",
    )(qt, k, vt)


def _moba_decode_kernel(pt_ref, qbd_ref, kn_ref, vn_ref, *refs, n_steps, pp, n_blocks, tq):
    k_refs = refs[:pp]
    v_refs = refs[pp:2 * pp]
    o_ref = refs[2 * pp]
    s_ref, g_ref, m_ref, l_ref, acc_ref = refs[2 * pp + 1:]
    del pt_ref
    s_id = pl.program_id(1)
    rows = tq * A_HEADS
    pages_per_block = MOBA_BLOCK // PAGE_SIZE
    qbd = qbd_ref[...]
    lane = lax.broadcasted_iota(jnp.int32, (rows, 128), 1)

    @pl.when(s_id == 0)
    def _():
        g_ref[...] = jnp.zeros_like(g_ref)

    @pl.when(s_id < n_steps)
    def _():
        part = None
        for pi in range(pp):
            page = s_id * pp + pi
            sc = _mm(qbd, k_refs[pi][...].astype(BF16))
            s_ref[:, pl.ds(pl.multiple_of(page * PAGE_SIZE, PAGE_SIZE), PAGE_SIZE)] = sc
            rs = jnp.sum(sc, axis=-1, keepdims=True)
            part = rs if pi % pages_per_block == 0 else part + rs
            if pi % pages_per_block == pages_per_block - 1:
                n = page // pages_per_block
                g_ref[...] = jnp.where(lane == n, part * (1.0 / MOBA_BLOCK), g_ref[...])

    @pl.when(s_id == n_steps - 1)
    def _():
        g = jnp.where(lane < n_blocks, g_ref[...], NEG_INF)
        sel = jnp.zeros(g.shape, F32)
        for _ in range(min(MOBA_TOPK, n_blocks)):
            mx = jnp.max(g, axis=-1, keepdims=True)
            idx = jnp.min(jnp.where(g == mx, lane, 128), axis=-1, keepdims=True)
            pick = lane == idx
            sel = jnp.where(pick, 1.0, sel)
            g = jnp.where(pick, NEG_INF, g)

        def mask_block(n, m):
            picked = jnp.max(jnp.where(lane == n, sel, 0.0), axis=-1, keepdims=True)
            b0 = pl.multiple_of(n * MOBA_BLOCK, MOBA_BLOCK)
            sb = jnp.where(picked > 0.0, s_ref[:, pl.ds(b0, MOBA_BLOCK)], NEG_INF)
            s_ref[:, pl.ds(b0, MOBA_BLOCK)] = sb
            return jnp.maximum(m, jnp.max(sb, axis=-1, keepdims=True))

        m_past = lax.fori_loop(0, n_blocks, mask_block, jnp.full((rows, 1), NEG_INF, F32),
                               unroll=next(u for u in (8, 4, 2, 1) if n_blocks % u == 0))
        sn = _nt(qbd, kn_ref[...])
        qi = lax.broadcasted_iota(jnp.int32, sn.shape, 0) // A_HEADS
        sn = jnp.where(lane <= qi, sn, NEG_INF)
        m = jnp.maximum(m_past, jnp.max(sn, axis=-1, keepdims=True))
        pn = jnp.exp2(sn - m)
        m_ref[...] = jnp.broadcast_to(m, m_ref.shape)
        l_ref[...] = jnp.broadcast_to(jnp.sum(pn, axis=-1, keepdims=True), l_ref.shape)
        acc_ref[...] = _mm(pn.astype(BF16), vn_ref[...])

    @pl.when(s_id >= n_steps)
    def _():
        m = m_ref[:, 0:1]
        lsum = jnp.zeros((rows, 1), F32)
        acc = jnp.zeros(acc_ref.shape, F32)
        for pi in range(pp):
            page = (s_id - n_steps) * pp + pi
            sc = s_ref[:, pl.ds(pl.multiple_of(page * PAGE_SIZE, PAGE_SIZE), PAGE_SIZE)]
            p = jnp.exp2(sc - m)
            lsum = lsum + jnp.sum(p, axis=-1, keepdims=True)
            acc = acc + _nt(p.astype(BF16), v_refs[pi][...].astype(BF16))
        l_ref[...] = l_ref[...] + lsum
        acc_ref[...] = acc_ref[...] + acc

    @pl.when(s_id == 2 * n_steps - 1)
    def _():
        o = acc_ref[...] / l_ref[:, 0:1]
        r_head = lax.broadcasted_iota(jnp.int32, o.shape, 0) % A_HEADS
        l_head = lax.broadcasted_iota(jnp.int32, o.shape, 1) // A_HD
        o = jnp.where(r_head == l_head, o, 0.0)
        o_ref[...] = jnp.zeros_like(o_ref)
        o_ref[0:tq, :] = jnp.sum(o.reshape(tq, A_HEADS, o.shape[-1]), axis=1)


def _moba_decode(q, k_new, v_new, cache_kt, cache_vt, page_table):
    b, tq, w = q.shape
    n_pages = page_table.shape[1]
    past = n_pages * PAGE_SIZE
    assert past % MOBA_BLOCK == 0 and tq <= PAGE_SIZE and tq <= MOBA_BLOCK
    n_blocks = past // MOBA_BLOCK
    assert n_blocks <= 128
    pp = DECODE_PAGES_PER_STEP
    assert n_pages % pp == 0 and pp % (MOBA_BLOCK // PAGE_SIZE) == 0
    n_steps = n_pages // pp
    rows = tq * A_HEADS
    head_of_lane = jnp.arange(w, dtype=jnp.int32) // A_HD
    head_mask = (head_of_lane[None, :] == jnp.arange(A_HEADS, dtype=jnp.int32)[:, None])
    qbd = jnp.where(head_mask[None, None], q[:, :, None, :], jnp.zeros((), q.dtype)).reshape(b, rows, w)
    pad = ((0, 0), (0, PAGE_SIZE - tq), (0, 0))
    kn = jnp.pad(k_new, pad)
    vn = jnp.pad(v_new, pad)

    def k_map(pi):
        return lambda bi, s, pt: (pt[bi, jnp.minimum(s, n_steps - 1) * pp + pi], 0, 0)

    def v_map(pi):
        return lambda bi, s, pt: (pt[bi, jnp.maximum(s - n_steps, 0) * pp + pi], 0, 0)

    per_b = lambda r: pl.BlockSpec((None, r, w), lambda bi, s, pt: (bi, 0, 0))
    page = lambda imap: pl.BlockSpec((None, w, PAGE_SIZE), imap)
    out_rows = 8 * pl.cdiv(tq, 8)
    grid_spec = pltpu.PrefetchScalarGridSpec(
        num_scalar_prefetch=1,
        grid=(b, 2 * n_steps),
        in_specs=[per_b(rows), per_b(PAGE_SIZE), per_b(PAGE_SIZE)]
        + [page(k_map(pi)) for pi in range(pp)] + [page(v_map(pi)) for pi in range(pp)],
        out_specs=per_b(out_rows),
        scratch_shapes=[pltpu.VMEM((rows, past), F32), pltpu.VMEM((rows, 128), F32),
                        pltpu.VMEM((rows, 128), F32), pltpu.VMEM((rows, 128), F32),
                        pltpu.VMEM((rows, w), F32)],
    )
    out = pl.pallas_call(
        functools.partial(_moba_decode_kernel, n_steps=n_steps, pp=pp, n_blocks=n_blocks, tq=tq),
        grid_spec=grid_spec,
        out_shape=jax.ShapeDtypeStruct((b, out_rows, w), F32),
        compiler_params=_cparams(("arbitrary", "arbitrary")),
        name="moba_decode",
    )(page_table, qbd, kn, vn, *([cache_kt] * pp), *([cache_vt] * pp))
    return out[:, :tq, :]


def _hgrn_kernel(qh_ref, lf_ref, kh_ref, v_ref, hg_ref, s0_ref, g_ref, o_ref, st_out_ref, st_ref, *, n_chunks):
    t = pl.program_id(1)
    c = HG_CHUNK

    @pl.when(t == 0)
    def _():
        for h in range(HG_HEADS):
            st_ref[h] = s0_ref[h].T

    row = lax.broadcasted_iota(jnp.int32, (c, HG_DK), 0)
    tril = lax.broadcasted_iota(jnp.int32, (c, c), 1) <= lax.broadcasted_iota(jnp.int32, (c, c), 0)
    for ci in range(n_chunks):
        rs = slice(ci * c, (ci + 1) * c)
        for h in range(HG_HEADS):
            ks = slice(h * HG_DK, (h + 1) * HG_DK)
            vs = slice(h * HG_DV, (h + 1) * HG_DV)
            b = lf_ref[rs, ks]
            sh = 1
            while sh < c:
                b = b + jnp.where(row >= sh, pltpu.roll(b, sh, axis=0), 0.0)
                sh *= 2
            ref = b[c // 2 - 1:c // 2, :]
            last = b[c - 1:c, :]
            qh = qh_ref[rs, ks].astype(F32)
            kh = kh_ref[rs, ks].astype(F32)
            v = v_ref[rs, vs]
            qe = (qh * jnp.exp(b - ref)).astype(BF16)
            ke = (kh * jnp.exp(ref - b)).astype(BF16)
            a = jnp.where(tril, _nt(qe, ke), 0.0)
            qs = (qh * jnp.exp(b)).astype(BF16)
            kd = (kh * jnp.exp(last - b)).astype(BF16)
            st = st_ref[h]
            o = _mm(a.astype(BF16), v) + _nt(qs, st.astype(BF16))
            st_ref[h] = st * jnp.exp(last) + _tn(v, kd)
            y = _rms(o, g_ref[:, vs]) * hg_ref[rs, vs].astype(F32)
            o_ref[rs, vs] = y.astype(BF16)

    @pl.when(t == pl.num_programs(1) - 1)
    def _():
        for h in range(HG_HEADS):
            st_out_ref[h] = st_ref[h].T


def _hgrn(qh, lf, kh, v, hg, s0, norm_g):
    b, t, w = qh.shape
    assert t % HG_CHUNK == 0
    rows = ROW_TILE if t % ROW_TILE == 0 else HG_CHUNK
    seq = pl.BlockSpec((None, rows, w), lambda bi, ti: (bi, ti, 0))
    state = pl.BlockSpec((None, HG_HEADS, HG_DK, HG_DV), lambda bi, ti: (bi, 0, 0, 0))
    return pl.pallas_call(
        functools.partial(_hgrn_kernel, n_chunks=rows // HG_CHUNK),
        grid=(b, t // rows),
        in_specs=[seq, seq, seq, seq, seq, state, pl.BlockSpec(norm_g.shape, lambda bi, ti: (0, 0))],
        out_specs=[seq, state],
        out_shape=[jax.ShapeDtypeStruct((b, t, w), BF16), jax.ShapeDtypeStruct(s0.shape, F32)],
        scratch_shapes=[pltpu.VMEM((HG_HEADS, HG_DV, HG_DK), F32)],
        compiler_params=_cparams(("arbitrary", "arbitrary")),
        name="hgrn",
    )(qh, lf, kh, v, hg, s0, norm_g)


def _mix_kernel(oa_ref, ob_ref, ga_ref, gb_ref, x_ref, g1_ref, sh2_ref, sc2_ref, n2_ref, wa_ref, wb_ref, wo_ref,
                x1_ref, h2_ref):
    ya = _mm(oa_ref[...], wa_ref[...])
    yb = _mm(ob_ref[...], wb_ref[...])
    mixed = ga_ref[...].astype(F32) * ya + gb_ref[...].astype(F32) * yb
    x1 = x_ref[...] + g1_ref[...] * _mm(mixed.astype(BF16), wo_ref[...])
    x1_ref[...] = x1
    h2_ref[...] = (_rms(x1, n2_ref[...]) * (1.0 + sc2_ref[...]) + sh2_ref[...]).astype(BF16)


def _mix_out(oa, ob, ga, gb, x2d, g1, sh2, sc2, norm2_g, wa, wb, wo, *, per_row, seq_len):
    m, d = x2d.shape
    tm = min(ROW_TILE, m)
    tps = max(seq_len // tm, 1)
    mod = _mod_spec(per_row, tm, tps, d)
    row = lambda width: pl.BlockSpec((tm, width), lambda i: (i, 0))
    full = lambda a: pl.BlockSpec(a.shape, lambda i: (0,) * a.ndim)
    return pl.pallas_call(
        _mix_kernel,
        grid=(m // tm,),
        in_specs=[row(A_WIDTH), row(HG_WIDTH), row(d), row(d), row(d), mod, mod, mod, full(norm2_g),
                  full(wa), full(wb), full(wo)],
        out_specs=[row(d), row(d)],
        out_shape=[jax.ShapeDtypeStruct((m, d), F32), jax.ShapeDtypeStruct((m, d), BF16)],
        compiler_params=_cparams(("arbitrary",)),
        name="mix_out",
    )(oa, ob, ga, gb, x2d, g1, sh2, sc2, norm2_g, wa, wb, wo)


def _gelu(x):
    return 0.5 * x * (1.0 + lax.erf(x * (2.0 ** -0.5)))


def _ffn_body(h2_ref, x1_ref, g2_ref, wup_ref, cw_ref, cb_ref, wdn_ref, fg_ref, y_ref, acc_ref, prev_rows, emit_u):
    tm, d_ff = h2_ref.shape[0], cw_ref.shape[1]
    h2 = h2_ref[...]
    acc_ref[...] = jnp.zeros_like(acc_ref)
    n_ft = d_ff // FF_TILE

    def up_proj(ft):
        return (_mm(h2, wup_ref[:, ft * FF_TILE:(ft + 1) * FF_TILE]),
                _mm(h2, wup_ref[:, d_ff + ft * FF_TILE:d_ff + (ft + 1) * FF_TILE]))

    pending = [up_proj(0)]
    for ft in range(n_ft):
        fs = slice(ft * FF_TILE, (ft + 1) * FF_TILE)
        if ft + 1 < n_ft:
            pending.append(up_proj(ft + 1))
        u, v = pending.pop(0)
        um1, um2 = prev_rows(u, fs)
        uc = cb_ref[:, fs] + cw_ref[0:1, fs] * um2 + cw_ref[1:2, fs] * um1 + cw_ref[2:3, fs] * u
        act = (_gelu(uc) * v).astype(BF16)
        acc_ref[...] += _mm(act, wdn_ref[fs, :])
        emit_u(u, fs)
    x2 = x1_ref[...] + g2_ref[...] * acc_ref[...]
    y_ref[...] = _rms(x2, fg_ref[...])


def _ffn_seq_kernel(h2_ref, x1_ref, g2_ref, cp_ref, wup_ref, cw_ref, cb_ref, wdn_ref, fg_ref,
                    y_ref, nc_ref, acc_ref, carry_ref):
    tm = h2_ref.shape[0]

    @pl.when(pl.program_id(1) == 0)
    def _():
        carry_ref[...] = cp_ref[...]

    row = lax.broadcasted_iota(jnp.int32, (tm, FF_TILE), 0)

    def prev_rows(u, fs):
        c0 = carry_ref[0:1, fs]
        c1 = carry_ref[1:2, fs]
        um1 = jnp.where(row == 0, c1, pltpu.roll(u, 1, axis=0))
        um2 = jnp.where(row == 0, c0, jnp.where(row == 1, c1, pltpu.roll(u, 2, axis=0)))
        return um1, um2

    def emit_u(u, fs):
        tail = u[tm - (CONV_W - 1):tm, :]
        carry_ref[:, fs] = tail
        nc_ref[:, fs] = tail

    _ffn_body(h2_ref, x1_ref, g2_ref, wup_ref, cw_ref, cb_ref, wdn_ref, fg_ref, y_ref, acc_ref, prev_rows, emit_u)


def _ffn_short_kernel(h2_ref, x1_ref, g2_ref, p1_ref, p2_ref, wup_ref, cw_ref, cb_ref, wdn_ref, fg_ref,
                      y_ref, u_ref, acc_ref, *, seq_len):
    tm = h2_ref.shape[0]
    tpos = lax.broadcasted_iota(jnp.int32, (tm, FF_TILE), 0) % seq_len

    def prev_rows(u, fs):
        um1 = jnp.where(tpos >= 1, pltpu.roll(u, 1, axis=0), p1_ref[:, fs])
        um2 = jnp.where(tpos >= 2, pltpu.roll(u, 2, axis=0), p2_ref[:, fs])
        return um1, um2

    def emit_u(u, fs):
        u_ref[:, fs] = u

    _ffn_body(h2_ref, x1_ref, g2_ref, wup_ref, cw_ref, cb_ref, wdn_ref, fg_ref, y_ref, acc_ref, prev_rows, emit_u)


def _ffn_seq(h2, x1, g2, conv_prev, wup, cw, cb, wdn, fg, *, batch, seq_len):
    m, d = x1.shape
    d_ff = cw.shape[1]
    tm = min(ROW_TILE, seq_len)
    assert seq_len % tm == 0 and d_ff % FF_TILE == 0 and tm >= CONV_W - 1
    tps = seq_len // tm
    row = lambda width: pl.BlockSpec((tm, width), lambda bi, ti: (bi * tps + ti, 0))
    full = lambda a: pl.BlockSpec(a.shape, lambda bi, ti: (0,) * a.ndim)
    per_b = lambda r, width: pl.BlockSpec((None, r, width), lambda bi, ti: (bi, 0, 0))
    return pl.pallas_call(
        _ffn_seq_kernel,
        grid=(batch, tps),
        in_specs=[row(d), row(d), per_b(1, d), per_b(CONV_W - 1, d_ff), full(wup), full(cw), full(cb), full(wdn),
                  full(fg)],
        out_specs=[row(d), per_b(CONV_W - 1, d_ff)],
        out_shape=[jax.ShapeDtypeStruct((m, d), F32), jax.ShapeDtypeStruct((batch, CONV_W - 1, d_ff), F32)],
        scratch_shapes=[pltpu.VMEM((tm, d), F32), pltpu.VMEM((CONV_W - 1, d_ff), F32)],
        compiler_params=_cparams(("arbitrary", "arbitrary")),
        name="conv_ffn",
    )(h2, x1, g2, conv_prev, wup, cw, cb, wdn, fg)


def _ffn_short(h2, x1, g2_rows, conv_prev, wup, cw, cb, wdn, fg, *, batch, seq_len):
    m, d = x1.shape
    d_ff = cw.shape[1]
    assert CONV_W == 3 and seq_len >= CONV_W - 1
    zeros = lambda n: jnp.zeros((batch, n, d_ff), F32)
    p1 = jnp.concatenate([conv_prev[:, 1:2], zeros(seq_len - 1)], axis=1).reshape(m, d_ff)
    p2 = jnp.concatenate([conv_prev[:, 0:2], zeros(seq_len - 2)], axis=1).reshape(m, d_ff)
    tm = m
    row = lambda width: pl.BlockSpec((tm, width), lambda i: (i, 0))
    full = lambda a: pl.BlockSpec(a.shape, lambda i: (0,) * a.ndim)
    y, u = pl.pallas_call(
        functools.partial(_ffn_short_kernel, seq_len=seq_len),
        grid=(m // tm,),
        in_specs=[row(d), row(d), pl.BlockSpec((None, tm, d), lambda i: (0, i, 0)), row(d_ff), row(d_ff),
                  full(wup), full(cw), full(cb), full(wdn), full(fg)],
        out_specs=[row(d), row(d_ff)],
        out_shape=[jax.ShapeDtypeStruct((m, d), F32), jax.ShapeDtypeStruct((m, d_ff), F32)],
        scratch_shapes=[pltpu.VMEM((tm, d), F32)],
        compiler_params=_cparams(("arbitrary",)),
        name="conv_ffn_short",
    )(h2, x1, g2_rows, p1, p2, wup, cw, cb, wdn, fg)
    new_conv = u.reshape(batch, seq_len, d_ff)[:, seq_len - (CONV_W - 1):, :]
    return y, new_conv


def kernel(x_prompt, x_sample, cache_k, cache_v, state_hgrn, state_conv, page_table, c_prompt, c_sample,
           norm1_g, norm2_g, w_ada, b_ada, w_in, hgrn_lb_logits, hg_norm_g, w_a_out, w_b_out, w_o,
           w_up, conv_w, conv_b, w_down, final_g):
    depth = w_in.shape[0]
    assert depth == 1 and hgrn_lb_logits.shape[0] == depth + 1
    bp, tp, d = x_prompt.shape
    bs, ts, _ = x_sample.shape
    d_ff = conv_w.shape[-1]
    w = A_WIDTH

    w_ada_bf = w_ada[0].astype(BF16)
    w_in_bf = w_in[0].astype(BF16)
    wa_bf = w_a_out[0].astype(BF16)
    wb_bf = w_b_out[0].astype(BF16)
    wo_bf = w_o[0].astype(BF16)
    wup_bf = w_up[0].astype(BF16)
    wdn_bf = w_down[0].astype(BF16)
    n1 = norm1_g[0].reshape(1, d)
    n2 = norm2_g[0].reshape(1, d)
    fg = final_g.reshape(1, d)
    hgn = hg_norm_g[0].reshape(1, HG_WIDTH)
    cw = conv_w[0]
    cb = conv_b[0].reshape(1, d_ff)

    mod = _ada_mod(jnp.concatenate([c_prompt, c_sample], axis=0), w_ada_bf, b_ada[0].reshape(1, 6 * d))

    mp = bp * tp
    modp = [mod[:bp, i * d:(i + 1) * d].reshape(bp, 1, d) for i in range(6)]
    xp = x_prompt.reshape(mp, d)
    (qt, kt32, vt32, kb, vtb, qh, lf, kh, hi, hg, ga, gb) = _in_proj(
        xp, modp[0], modp[1], n1, w_in_bf, hgrn_lb_logits, per_row=False, seq_len=tp)
    seq3 = lambda a: a.reshape(bp, tp, w)
    oa = _moba_prompt(qt, seq3(kb), vtb).reshape(mp, w)
    ob, sp = _hgrn(seq3(qh), seq3(lf), seq3(kh), seq3(hi), seq3(hg),
                   jnp.zeros((bp, HG_HEADS, HG_DK, HG_DV), F32), hgn)
    x1, h2 = _mix_out(oa, ob.reshape(mp, w), ga, gb, xp, modp[2], modp[3], modp[4], n2, wa_bf, wb_bf, wo_bf,
                      per_row=False, seq_len=tp)
    yp, cp = _ffn_seq(h2, x1, modp[5], jnp.zeros((bp, CONV_W - 1, d_ff), F32), wup_bf, cw, cb, wdn_bf, fg,
                      batch=bp, seq_len=tp)

    ms = bs * ts
    mods_rows = jnp.repeat(mod[bp:], ts, axis=0)
    mods = [mods_rows[:, i * d:(i + 1) * d].reshape(1, ms, d) for i in range(6)]
    xs = x_sample.reshape(ms, d)
    (q_s, k32_s, v32_s, kb_s, vb_s, qh_s, lf_s, kh_s, hi_s, hg_s, ga_s, gb_s) = _in_proj(
        xs, mods[0], mods[1], n1, w_in_bf, hgrn_lb_logits, per_row=True, seq_len=ts)
    seq3s = lambda a: a.reshape(bs, ts, w)
    n_phys = cache_k.shape[1]
    page_t = lambda c: jnp.transpose(c[0], (0, 2, 3, 1)).reshape(n_phys, w, PAGE_SIZE)
    oa_s = _moba_decode(seq3s(q_s), seq3s(kb_s), seq3s(vb_s), page_t(cache_k), page_t(cache_v), page_table)
    t_pad = HG_CHUNK * pl.cdiv(ts, HG_CHUNK)
    padt = lambda a: jnp.pad(seq3s(a), ((0, 0), (0, t_pad - ts), (0, 0)))
    ob_s, ss = _hgrn(padt(qh_s), padt(lf_s), padt(kh_s), padt(hi_s), padt(hg_s), state_hgrn[0], hgn)
    ob_s = ob_s[:, :ts, :].reshape(ms, w)
    x1_s, h2_s = _mix_out(oa_s.reshape(ms, w).astype(BF16), ob_s, ga_s, gb_s, xs, mods[2], mods[3], mods[4], n2,
                          wa_bf, wb_bf, wo_bf, per_row=True, seq_len=ts)
    ys, cs = _ffn_short(h2_s, x1_s, mods[5], state_conv[0], wup_bf, cw, cb, wdn_bf, fg, batch=bs, seq_len=ts)

    kv_p = lambda a: jnp.transpose(a.reshape(1, bp, A_HEADS, A_HD, tp), (0, 1, 4, 2, 3))
    kv_s = lambda a: a.reshape(1, bs, ts, A_HEADS, A_HD)
    return (yp.reshape(bp, tp, d), ys.reshape(bs, ts, d), kv_p(kt32), kv_p(vt32), kv_s(k32_s), kv_s(v32_s),
            sp[None], ss[None], cp[None], cs[None])
```

```python
import functools

import jax
import jax.numpy as jnp
from jax import lax
from jax.experimental import pallas as pl
from jax.experimental.pallas import tpu as pltpu

F32 = jnp.float32
BF16 = jnp.bfloat16

A_HEADS = 8
A_HD = 64
A_WIDTH = A_HEADS * A_HD
MOBA_BLOCK = 256
MOBA_TOPK = 3
GATE_GROUP = 16
VALUE_ROWS = A_HD + 16
HG_HEADS = 4
HG_DK = 128
HG_DV = 128
HG_WIDTH = HG_HEADS * HG_DV
HG_CHUNK = 64
CONV_W = 3
PAGE_SIZE = 128
EPS = 1e-6
NEG_INF = float("-inf")
MASK_BIAS = -1e30
LOG2E = 1.4426950408889634

VMEM_LIMIT_BYTES = 56 * 1024 * 1024
ROW_TILE = 512
FF_TILE = 256
FFN_ROW_TILE = 512
DECODE_PAGES_PER_STEP = 32
SCORE_LOOKAHEAD = 6


def _cparams(semantics, flags=None):
    return pltpu.CompilerParams(dimension_semantics=semantics, vmem_limit_bytes=VMEM_LIMIT_BYTES, flags=flags)


def _nt(a, b):
    return lax.dot_general(a, b, (((1,), (1,)), ((), ())), preferred_element_type=F32)


def _tn(a, b):
    return lax.dot_general(a, b, (((0,), (0,)), ((), ())), preferred_element_type=F32)


def _mm(a, b):
    return jnp.dot(a, b, preferred_element_type=F32)


def _rms(x, g):
    return x * lax.rsqrt(jnp.mean(x * x, axis=-1, keepdims=True) + EPS) * g


def _ada_kernel(c_ref, w_ref, b_ref, o_ref):
    c = c_ref[...]
    s = c * jax.nn.sigmoid(c)
    o_ref[...] = _mm(s.astype(BF16), w_ref[...]) + b_ref[...]


def _ada_mod(c, w_bf, b):
    n, d = c.shape
    cols = w_bf.shape[1]
    tn = 1536
    assert cols % tn == 0
    return pl.pallas_call(
        _ada_kernel,
        grid=(cols // tn,),
        in_specs=[pl.BlockSpec((n, d), lambda j: (0, 0)),
                  pl.BlockSpec((d, tn), lambda j: (0, j)),
                  pl.BlockSpec((1, tn), lambda j: (0, j))],
        out_specs=pl.BlockSpec((n, tn), lambda j: (0, j)),
        out_shape=jax.ShapeDtypeStruct((n, cols), F32),
        compiler_params=_cparams(("arbitrary",)),
        name="ada_mod",
    )(c, w_bf, b)


def _mod_spec(per_row, tm, tiles_per_seq, d):
    if per_row:
        return pl.BlockSpec((None, tm, d), lambda i: (0, i, 0))
    return pl.BlockSpec((None, 1, d), lambda i: (i // tiles_per_seq, 0, 0))


def _inproj_kernel(x_ref, sh_ref, sc_ref, g_ref, w_ref, lbl_ref,
                   q_ref, k32_ref, v32_ref, kb_ref, vb_ref, qh_ref, lf_ref, kh_ref, hi_ref, hg_ref,
                   ga_ref, gb_ref, hb_ref, *, d_model, kv_transposed):
    hb_ref[...] = (_rms(x_ref[...], g_ref[...]) * (1.0 + sc_ref[...]) + sh_ref[...]).astype(BF16)
    w = A_WIDTH

    def ep_q(q):
        q = q * (A_HD ** -0.5 * LOG2E)
        q_ref[...] = (q.T if kv_transposed else q).astype(BF16)

    def ep_k(k):
        kb_ref[...] = k.astype(BF16)
        k32_ref[...] = k.T if kv_transposed else k

    def ep_v(v):
        v = v.T if kv_transposed else v
        v32_ref[...] = v
        vb_ref[...] = v.astype(BF16)

    def ep_hq(hq):
        qh_ref[...] = (hq * jax.nn.sigmoid(hq)).astype(BF16)

    def ep_hf(hf):
        lbl = lbl_ref[...]
        e = jnp.exp(lbl - jnp.max(lbl, axis=0, keepdims=True))
        lb = e[0:1, :] / jnp.sum(e, axis=0, keepdims=True)
        en = jnp.exp(-jnp.abs(hf))
        r = 1.0 / (1.0 + en)
        pos = hf >= 0.0
        sg = jnp.where(pos, r, en * r)
        sgn = jnp.where(pos, en * r, r)
        lf_ref[...] = jnp.log(lb + (1.0 - lb) * sg)
        kh_ref[...] = ((1.0 - lb) * sgn).astype(BF16)

    def ep_hi(hi):
        hi_ref[...] = hi.astype(BF16)

    def ep_hg(hg):
        hg_ref[...] = (hg * jax.nn.sigmoid(hg)).astype(BF16)

    def ep_ga(ga):
        ga_ref[...] = jax.nn.sigmoid(ga).astype(BF16)

    def ep_gb(gb):
        gb_ref[...] = jax.nn.sigmoid(gb).astype(BF16)

    groups = [(w, ep_q), (w, ep_k), (w, ep_v), (w, ep_hq), (w, ep_hf), (w, ep_hi), (w, ep_hg),
              (d_model, ep_ga), (d_model, ep_gb)]
    starts = [sum(width for width, _ in groups[:n]) for n in range(len(groups))]
    for n in (7, 8, 4, 3, 6, 0, 1, 2, 5):
        width, epilogue = groups[n]
        epilogue(_mm(hb_ref[...], w_ref[:, starts[n]:starts[n] + width]))


def _in_proj(x2d, sh1, sc1, norm_g, w_in_bf, lb_logits, *, per_row, seq_len):
    m, d = x2d.shape
    tm = min(ROW_TILE, m)
    assert m % tm == 0 and (per_row or seq_len % tm == 0)
    tps = max(seq_len // tm, 1)
    mod = _mod_spec(per_row, tm, tps, d)
    row = lambda width: pl.BlockSpec((tm, width), lambda i: (i, 0))
    full = lambda a: pl.BlockSpec(a.shape, lambda i: (0,) * a.ndim)
    w = A_WIDTH
    outs = [(w, BF16), (w, F32), (w, F32), (w, BF16), (w, BF16), (w, BF16), (w, F32), (w, BF16), (w, BF16),
            (w, BF16), (d, BF16), (d, BF16)]
    out_specs = [row(width) for width, _ in outs]
    out_shape = [jax.ShapeDtypeStruct((m, width), dt) for width, dt in outs]
    kv_transposed = not per_row
    if kv_transposed:
        for idx in (0, 1, 2, 4):
            out_specs[idx] = pl.BlockSpec((None, w, tm), lambda i: (i // tps, 0, i % tps))
            out_shape[idx] = jax.ShapeDtypeStruct((m // seq_len, w, seq_len), outs[idx][1])
    return pl.pallas_call(
        functools.partial(_inproj_kernel, d_model=d, kv_transposed=kv_transposed),
        grid=(m // tm,),
        in_specs=[row(d), mod, mod, full(norm_g), full(w_in_bf), full(lb_logits)],
        out_specs=out_specs,
        out_shape=out_shape,
        scratch_shapes=[pltpu.VMEM((tm, d), BF16)],
        compiler_params=_cparams(("arbitrary",)),
        name="in_proj",
    )(x2d, sh1, sc1, norm_g, w_in_bf, lb_logits)


def _moba_prompt_kernel(qt_ref, k_ref, vt_ref, o_ref, km_ref, kmbd_ref, kaug_ref, vaug_ref, qaug_ref, gate_ref, m_ref,
                        acc_ref, *, nb):
    i = pl.program_id(1)
    blk = MOBA_BLOCK
    gg = GATE_GROUP
    vr = VALUE_ROWS
    lane = lax.broadcasted_iota(jnp.int32, (blk, 128), 1)

    def head_lanes(x, h):
        part = x[:, (h // 2) * 128:(h // 2 + 1) * 128]
        return pltpu.roll(part, A_HD, axis=1) if h % 2 else part

    @pl.when(i == 0)
    def _():
        qaug_ref[...] = jnp.zeros_like(qaug_ref)
        km_ref[...] = jnp.zeros_like(km_ref)
        for n in range(nb):
            rs = slice(n * blk, (n + 1) * blk)
            kf = k_ref[rs, :].astype(F32)
            km_ref[n:n + 1, :] = jnp.mean(kf, axis=0, keepdims=True)
            onehot = jnp.where(lane == A_HD + n, 1.0, 0.0)
            for h in range(A_HEADS):
                kaug_ref[rs, h * 128:(h + 1) * 128] = jnp.where(lane < A_HD, head_lanes(kf, h), onehot).astype(BF16)
        km = km_ref[...]
        lane_head = lax.broadcasted_iota(jnp.int32, km.shape, 1) // A_HD
        ones_row = jnp.where(lax.broadcasted_iota(jnp.int32, (vr - A_HD, vt_ref.shape[1]), 0) == 0, 1.0, 0.0)
        for h in range(A_HEADS):
            kmbd_ref[h * gg:(h + 1) * gg, :] = jnp.where(lane_head == h, km, 0.0).astype(BF16)
            vaug_ref[h * vr:h * vr + A_HD, :] = vt_ref[h * A_HD:(h + 1) * A_HD, :]
            vaug_ref[h * vr + A_HD:(h + 1) * vr, :] = ones_row.astype(BF16)

    qt = qt_ref[...]
    gate_ref[...] = _mm(kmbd_ref[...], qt)
    n_id = lax.broadcasted_iota(jnp.int32, (gg, blk), 0)
    for h in range(A_HEADS):
        g_h = gate_ref[h * gg:(h + 1) * gg, :]
        rank = jnp.zeros(g_h.shape, jnp.int32)
        for m in range(nb):
            g_m = gate_ref[h * gg + m:h * gg + m + 1, :]
            beats = (g_m > g_h) | ((g_m == g_h) & (m < n_id))
            rank = rank + jnp.where(beats, jnp.where(m < i, 1, 0), 0)
        bias_h = jnp.where((n_id < i) & (rank >= MOBA_TOPK), MASK_BIAS, 0.0)
        qaug_ref[h, 0:A_HD, :] = qt[h * A_HD:(h + 1) * A_HD, :]
        qaug_ref[h, A_HD:A_HD + gg, :] = bias_h.astype(BF16)

    def scores(j0, h):
        return _mm(kaug_ref[pl.ds(j0, blk), h * 128:(h + 1) * 128], qaug_ref[h])

    own0 = pl.multiple_of(i * blk, blk)
    causal = (lax.broadcasted_iota(jnp.int32, (blk, blk), 0) <= lax.broadcasted_iota(jnp.int32, (blk, blk), 1))

    def own_scores(h):
        return jnp.where(causal, scores(own0, h), NEG_INF)

    def accumulate(j0, s, h, first):
        hr = slice(h * vr, (h + 1) * vr)
        m_blk = jnp.max(s, axis=0, keepdims=True)
        if first:
            m_new = m_blk
        else:
            m_old = m_ref[h, 0:1, :]
            m_new = jnp.maximum(m_old, m_blk)
            alpha = jnp.exp2(m_old - m_new)
        p = jnp.exp2(s - m_new)
        pv = _mm(vaug_ref[hr, pl.ds(j0, blk)], p.astype(BF16))
        m_ref[h, 0:1, :] = m_new
        acc_ref[hr, :] = pv if first else alpha * acc_ref[hr, :] + pv

    def run_blocks(starts, score_fn, first):
        units = [(j0, h) for j0 in starts for h in range(A_HEADS)]
        pending = [score_fn(*u) for u in units[:SCORE_LOOKAHEAD]]
        for n, (j0, h) in enumerate(units):
            if n + SCORE_LOOKAHEAD < len(units):
                pending.append(score_fn(*units[n + SCORE_LOOKAHEAD]))
            accumulate(j0, pending.pop(0), h, first)

    run_blocks([own0], lambda j0, h: own_scores(h), True)

    def pair_body(t, carry):
        j0 = pl.multiple_of(t * (2 * blk), 2 * blk)
        run_blocks([j0, j0 + blk], scores, False)
        return carry

    lax.fori_loop(0, i // 2, pair_body, 0)

    @pl.when(i % 2 == 1)
    def _():
        run_blocks([pl.multiple_of((i - 1) * blk, blk)], scores, False)

    out_t = jnp.concatenate(
        [acc_ref[h * vr:h * vr + A_HD, :] / acc_ref[h * vr + A_HD:h * vr + A_HD + 1, :] for h in range(A_HEADS)], axis=0)
    o_ref[...] = out_t.T.astype(BF16)


def _moba_prompt(qt, k, vt):
    b, t, w = k.shape
    assert t % MOBA_BLOCK == 0
    nb = t // MOBA_BLOCK
    assert nb <= GATE_GROUP
    return pl.pallas_call(
        functools.partial(_moba_prompt_kernel, nb=nb),
        grid=(b, nb),
        in_specs=[pl.BlockSpec((None, w, MOBA_BLOCK), lambda bi, i: (bi, 0, i)),
                  pl.BlockSpec((None, t, w), lambda bi, i: (bi, 0, 0)),
                  pl.BlockSpec((None, w, t), lambda bi, i: (bi, 0, 0))],
        out_specs=pl.BlockSpec((None, MOBA_BLOCK, w), lambda bi, i: (bi, i, 0)),
        out_shape=jax.ShapeDtypeStruct((b, t, w), BF16),
        scratch_shapes=[pltpu.VMEM((GATE_GROUP, w), F32), pltpu.VMEM((A_HEADS * GATE_GROUP, w), BF16),
                        pltpu.VMEM((t, A_HEADS * 128), BF16),
                        pltpu.VMEM((A_HEADS * VALUE_ROWS, t), BF16),
                        pltpu.VMEM((A_HEADS, 128, MOBA_BLOCK), BF16),
                        pltpu.VMEM((A_HEADS * GATE_GROUP, MOBA_BLOCK), F32),
                        pltpu.VMEM((A_HEADS, 8, MOBA_BLOCK), F32),
                        pltpu.VMEM((A_HEADS * VALUE_ROWS, MOBA_BLOCK), F32)],
        compiler_params=_cparams(("arbitrary", "arbitrary")),
        name="moba_prompt",
    )(qt, k, vt)


def _moba_decode_kernel(pt_ref, qbd_ref, kn_ref, vn_ref, *refs, n_steps, pp, n_blocks, tq):
    k_refs = refs[:pp]
    v_refs = refs[pp:2 * pp]
    o_ref = refs[2 * pp]
    s_ref, g_ref, m_ref, l_ref, acc_ref = refs[2 * pp + 1:]
    del pt_ref
    s_id = pl.program_id(1)
    rows = tq * A_HEADS
    pages_per_block = MOBA_BLOCK // PAGE_SIZE
    qbd = qbd_ref[...]
    lane = lax.broadcasted_iota(jnp.int32, (rows, 128), 1)

    @pl.when(s_id == 0)
    def _():
        g_ref[...] = jnp.zeros_like(g_ref)

    @pl.when(s_id < n_steps)
    def _():
        part = None
        for pi in range(pp):
            page = s_id * pp + pi
            sc = _mm(qbd, k_refs[pi][...].astype(BF16))
            s_ref[:, pl.ds(pl.multiple_of(page * PAGE_SIZE, PAGE_SIZE), PAGE_SIZE)] = sc
            rs = jnp.sum(sc, axis=-1, keepdims=True)
            part = rs if pi % pages_per_block == 0 else part + rs
            if pi % pages_per_block == pages_per_block - 1:
                n = page // pages_per_block
                g_ref[...] = jnp.where(lane == n, part * (1.0 / MOBA_BLOCK), g_ref[...])

    @pl.when(s_id == n_steps - 1)
    def _():
        g = jnp.where(lane < n_blocks, g_ref[...], NEG_INF)
        sel = jnp.zeros(g.shape, F32)
        for _ in range(min(MOBA_TOPK, n_blocks)):
            mx = jnp.max(g, axis=-1, keepdims=True)
            idx = jnp.min(jnp.where(g == mx, lane, 128), axis=-1, keepdims=True)
            pick = lane == idx
            sel = jnp.where(pick, 1.0, sel)
            g = jnp.where(pick, NEG_INF, g)

        def mask_block(n, m):
            picked = jnp.max(jnp.where(lane == n, sel, 0.0), axis=-1, keepdims=True)
            b0 = pl.multiple_of(n * MOBA_BLOCK, MOBA_BLOCK)
            sb = jnp.where(picked > 0.0, s_ref[:, pl.ds(b0, MOBA_BLOCK)], NEG_INF)
            s_ref[:, pl.ds(b0, MOBA_BLOCK)] = sb
            return jnp.maximum(m, jnp.max(sb, axis=-1, keepdims=True))

        m_past = lax.fori_loop(0, n_blocks, mask_block, jnp.full((rows, 1), NEG_INF, F32),
                               unroll=next(u for u in (8, 4, 2, 1) if n_blocks % u == 0))
        sn = _nt(qbd, kn_ref[...])
        qi = lax.broadcasted_iota(jnp.int32, sn.shape, 0) // A_HEADS
        sn = jnp.where(lane <= qi, sn, NEG_INF)
        m = jnp.maximum(m_past, jnp.max(sn, axis=-1, keepdims=True))
        pn = jnp.exp2(sn - m)
        m_ref[...] = jnp.broadcast_to(m, m_ref.shape)
        l_ref[...] = jnp.broadcast_to(jnp.sum(pn, axis=-1, keepdims=True), l_ref.shape)
        acc_ref[...] = _mm(pn.astype(BF16), vn_ref[...])

    @pl.when(s_id >= n_steps)
    def _():
        m = m_ref[:, 0:1]
        lsum = jnp.zeros((rows, 1), F32)
        acc = jnp.zeros(acc_ref.shape, F32)
        for pi in range(pp):
            page = (s_id - n_steps) * pp + pi
            sc = s_ref[:, pl.ds(pl.multiple_of(page * PAGE_SIZE, PAGE_SIZE), PAGE_SIZE)]
            p = jnp.exp2(sc - m)
            lsum = lsum + jnp.sum(p, axis=-1, keepdims=True)
            acc = acc + _nt(p.astype(BF16), v_refs[pi][...].astype(BF16))
        l_ref[...] = l_ref[...] + lsum
        acc_ref[...] = acc_ref[...] + acc

    @pl.when(s_id == 2 * n_steps - 1)
    def _():
        o = acc_ref[...] / l_ref[:, 0:1]
        r_head = lax.broadcasted_iota(jnp.int32, o.shape, 0) % A_HEADS
        l_head = lax.broadcasted_iota(jnp.int32, o.shape, 1) // A_HD
        o = jnp.where(r_head == l_head, o, 0.0)
        o_ref[...] = jnp.zeros_like(o_ref)
        o_ref[0:tq, :] = jnp.sum(o.reshape(tq, A_HEADS, o.shape[-1]), axis=1)


def _moba_decode(q, k_new, v_new, cache_kt, cache_vt, page_table):
    b, tq, w = q.shape
    n_pages = page_table.shape[1]
    past = n_pages * PAGE_SIZE
    assert past % MOBA_BLOCK == 0 and tq <= PAGE_SIZE and tq <= MOBA_BLOCK
    n_blocks = past // MOBA_BLOCK
    assert n_blocks <= 128
    pp = DECODE_PAGES_PER_STEP
    assert n_pages % pp == 0 and pp % (MOBA_BLOCK // PAGE_SIZE) == 0
    n_steps = n_pages // pp
    rows = tq * A_HEADS
    head_of_lane = jnp.arange(w, dtype=jnp.int32) // A_HD
    head_mask = (head_of_lane[None, :] == jnp.arange(A_HEADS, dtype=jnp.int32)[:, None])
    qbd = jnp.where(head_mask[None, None], q[:, :, None, :], jnp.zeros((), q.dtype)).reshape(b, rows, w)
    pad = ((0, 0), (0, PAGE_SIZE - tq), (0, 0))
    kn = jnp.pad(k_new, pad)
    vn = jnp.pad(v_new, pad)

    def k_map(pi):
        return lambda bi, s, pt: (pt[bi, jnp.minimum(s, n_steps - 1) * pp + pi], 0, 0)

    def v_map(pi):
        return lambda bi, s, pt: (pt[bi, jnp.maximum(s - n_steps, 0) * pp + pi], 0, 0)

    per_b = lambda r: pl.BlockSpec((None, r, w), lambda bi, s, pt: (bi, 0, 0))
    page = lambda imap: pl.BlockSpec((None, w, PAGE_SIZE), imap)
    out_rows = 8 * pl.cdiv(tq, 8)
    grid_spec = pltpu.PrefetchScalarGridSpec(
        num_scalar_prefetch=1,
        grid=(b, 2 * n_steps),
        in_specs=[per_b(rows), per_b(PAGE_SIZE), per_b(PAGE_SIZE)]
        + [page(k_map(pi)) for pi in range(pp)] + [page(v_map(pi)) for pi in range(pp)],
        out_specs=per_b(out_rows),
        scratch_shapes=[pltpu.VMEM((rows, past), F32), pltpu.VMEM((rows, 128), F32),
                        pltpu.VMEM((rows, 128), F32), pltpu.VMEM((rows, 128), F32),
                        pltpu.VMEM((rows, w), F32)],
    )
    out = pl.pallas_call(
        functools.partial(_moba_decode_kernel, n_steps=n_steps, pp=pp, n_blocks=n_blocks, tq=tq),
        grid_spec=grid_spec,
        out_shape=jax.ShapeDtypeStruct((b, out_rows, w), F32),
        compiler_params=_cparams(("arbitrary", "arbitrary")),
        name="moba_decode",
    )(page_table, qbd, kn, vn, *([cache_kt] * pp), *([cache_vt] * pp))
    return out[:, :tq, :]


def _hgrn_kernel(qh_ref, lf_ref, kh_ref, v_ref, hg_ref, s0_ref, g_ref, o_ref, st_out_ref, st_ref, *, n_chunks):
    t = pl.program_id(1)
    c = HG_CHUNK

    @pl.when(t == 0)
    def _():
        for h in range(HG_HEADS):
            st_ref[h] = s0_ref[h].T

    row = lax.broadcasted_iota(jnp.int32, (c, HG_DK), 0)
    tril = lax.broadcasted_iota(jnp.int32, (c, c), 1) <= lax.broadcasted_iota(jnp.int32, (c, c), 0)
    for ci in range(n_chunks):
        rs = slice(ci * c, (ci + 1) * c)
        for h in range(HG_HEADS):
            ks = slice(h * HG_DK, (h + 1) * HG_DK)
            vs = slice(h * HG_DV, (h + 1) * HG_DV)
            b = lf_ref[rs, ks]
            sh = 1
            while sh < c:
                b = b + jnp.where(row >= sh, pltpu.roll(b, sh, axis=0), 0.0)
                sh *= 2
            ref = b[c // 2 - 1:c // 2, :]
            last = b[c - 1:c, :]
            qh = qh_ref[rs, ks].astype(F32)
            kh = kh_ref[rs, ks].astype(F32)
            v = v_ref[rs, vs]
            qe = (qh * jnp.exp(b - ref)).astype(BF16)
            ke = (kh * jnp.exp(ref - b)).astype(BF16)
            a = jnp.where(tril, _nt(qe, ke), 0.0)
            qs = (qh * jnp.exp(b)).astype(BF16)
            kd = (kh * jnp.exp(last - b)).astype(BF16)
            st = st_ref[h]
            o = _mm(a.astype(BF16), v) + _nt(qs, st.astype(BF16))
            st_ref[h] = st * jnp.exp(last) + _tn(v, kd)
            y = _rms(o, g_ref[:, vs]) * hg_ref[rs, vs].astype(F32)
            o_ref[rs, vs] = y.astype(BF16)

    @pl.when(t == pl.num_programs(1) - 1)
    def _():
        for h in range(HG_HEADS):
            st_out_ref[h] = st_ref[h].T


def _hgrn(qh, lf, kh, v, hg, s0, norm_g):
    b, t, w = qh.shape
    assert t % HG_CHUNK == 0
    rows = ROW_TILE if t % ROW_TILE == 0 else HG_CHUNK
    seq = pl.BlockSpec((None, rows, w), lambda bi, ti: (bi, ti, 0))
    state = pl.BlockSpec((None, HG_HEADS, HG_DK, HG_DV), lambda bi, ti: (bi, 0, 0, 0))
    return pl.pallas_call(
        functools.partial(_hgrn_kernel, n_chunks=rows // HG_CHUNK),
        grid=(b, t // rows),
        in_specs=[seq, seq, seq, seq, seq, state, pl.BlockSpec(norm_g.shape, lambda bi, ti: (0, 0))],
        out_specs=[seq, state],
        out_shape=[jax.ShapeDtypeStruct((b, t, w), BF16), jax.ShapeDtypeStruct(s0.shape, F32)],
        scratch_shapes=[pltpu.VMEM((HG_HEADS, HG_DV, HG_DK), F32)],
        compiler_params=_cparams(("arbitrary", "arbitrary")),
        name="hgrn",
    )(qh, lf, kh, v, hg, s0, norm_g)


def _mix_kernel(oa_ref, ob_ref, ga_ref, gb_ref, x_ref, g1_ref, sh2_ref, sc2_ref, n2_ref, wa_ref, wb_ref, wo_ref,
                x1_ref, h2_ref):
    ya = _mm(oa_ref[...], wa_ref[...])
    yb = _mm(ob_ref[...], wb_ref[...])
    mixed = ga_ref[...].astype(F32) * ya + gb_ref[...].astype(F32) * yb
    x1 = x_ref[...] + g1_ref[...] * _mm(mixed.astype(BF16), wo_ref[...])
    x1_ref[...] = x1
    h2_ref[...] = (_rms(x1, n2_ref[...]) * (1.0 + sc2_ref[...]) + sh2_ref[...]).astype(BF16)


def _mix_out(oa, ob, ga, gb, x2d, g1, sh2, sc2, norm2_g, wa, wb, wo, *, per_row, seq_len):
    m, d = x2d.shape
    tm = min(ROW_TILE, m)
    tps = max(seq_len // tm, 1)
    mod = _mod_spec(per_row, tm, tps, d)
    row = lambda width: pl.BlockSpec((tm, width), lambda i: (i, 0))
    full = lambda a: pl.BlockSpec(a.shape, lambda i: (0,) * a.ndim)
    return pl.pallas_call(
        _mix_kernel,
        grid=(m // tm,),
        in_specs=[row(A_WIDTH), row(HG_WIDTH), row(d), row(d), row(d), mod, mod, mod, full(norm2_g),
                  full(wa), full(wb), full(wo)],
        out_specs=[row(d), row(d)],
        out_shape=[jax.ShapeDtypeStruct((m, d), F32), jax.ShapeDtypeStruct((m, d), BF16)],
        compiler_params=_cparams(("arbitrary",)),
        name="mix_out",
    )(oa, ob, ga, gb, x2d, g1, sh2, sc2, norm2_g, wa, wb, wo)


def _gelu(x):
    return 0.5 * x * (1.0 + lax.erf(x * (2.0 ** -0.5)))


def _ffn_body(h2_ref, x1_ref, g2_ref, wup_ref, cw_ref, cb_ref, wdn_ref, fg_ref, y_ref, act_ref, prev_rows, emit_u):
    d_ff = cw_ref.shape[1]
    h2 = h2_ref[...]
    for ft in range(d_ff // FF_TILE):
        fs = slice(ft * FF_TILE, (ft + 1) * FF_TILE)
        u = _mm(h2, wup_ref[:, fs])
        v = _mm(h2, wup_ref[:, d_ff + ft * FF_TILE:d_ff + (ft + 1) * FF_TILE])
        um1, um2 = prev_rows(u, fs)
        uc = cb_ref[:, fs] + cw_ref[0:1, fs] * um2 + cw_ref[1:2, fs] * um1 + cw_ref[2:3, fs] * u
        act_ref[:, fs] = (_gelu(uc) * v).astype(BF16)
        emit_u(u, fs)
    x2 = x1_ref[...] + g2_ref[...] * _mm(act_ref[...], wdn_ref[...])
    y_ref[...] = _rms(x2, fg_ref[...])


def _ffn_seq_kernel(h2_ref, x1_ref, g2_ref, cp_ref, wup_ref, cw_ref, cb_ref, wdn_ref, fg_ref,
                    y_ref, nc_ref, act_ref, carry_ref):
    tm = h2_ref.shape[0]

    @pl.when(pl.program_id(1) == 0)
    def _():
        carry_ref[...] = cp_ref[...]

    row = lax.broadcasted_iota(jnp.int32, (tm, FF_TILE), 0)

    def prev_rows(u, fs):
        c0 = carry_ref[0:1, fs]
        c1 = carry_ref[1:2, fs]
        um1 = jnp.where(row == 0, c1, pltpu.roll(u, 1, axis=0))
        um2 = jnp.where(row == 0, c0, jnp.where(row == 1, c1, pltpu.roll(u, 2, axis=0)))
        return um1, um2

    def emit_u(u, fs):
        tail = u[tm - (CONV_W - 1):tm, :]
        carry_ref[:, fs] = tail
        nc_ref[:, fs] = tail

    _ffn_body(h2_ref, x1_ref, g2_ref, wup_ref, cw_ref, cb_ref, wdn_ref, fg_ref, y_ref, act_ref, prev_rows, emit_u)


def _ffn_short_kernel(h2_ref, x1_ref, g2_ref, p1_ref, p2_ref, wup_ref, cw_ref, cb_ref, wdn_ref, fg_ref,
                      y_ref, u_ref, act_ref, *, seq_len):
    tm = h2_ref.shape[0]
    tpos = lax.broadcasted_iota(jnp.int32, (tm, FF_TILE), 0) % seq_len

    def prev_rows(u, fs):
        um1 = jnp.where(tpos >= 1, pltpu.roll(u, 1, axis=0), p1_ref[:, fs])
        um2 = jnp.where(tpos >= 2, pltpu.roll(u, 2, axis=0), p2_ref[:, fs])
        return um1, um2

    def emit_u(u, fs):
        u_ref[:, fs] = u

    _ffn_body(h2_ref, x1_ref, g2_ref, wup_ref, cw_ref, cb_ref, wdn_ref, fg_ref, y_ref, act_ref, prev_rows, emit_u)


def _ffn_seq(h2, x1, g2, conv_prev, wup, cw, cb, wdn, fg, *, batch, seq_len):
    m, d = x1.shape
    d_ff = cw.shape[1]
    tm = min(FFN_ROW_TILE, seq_len)
    assert seq_len % tm == 0 and d_ff % FF_TILE == 0 and tm >= CONV_W - 1
    tps = seq_len // tm
    row = lambda width: pl.BlockSpec((tm, width), lambda bi, ti: (bi * tps + ti, 0))
    full = lambda a: pl.BlockSpec(a.shape, lambda bi, ti: (0,) * a.ndim)
    per_b = lambda r, width: pl.BlockSpec((None, r, width), lambda bi, ti: (bi, 0, 0))
    return pl.pallas_call(
        _ffn_seq_kernel,
        grid=(batch, tps),
        in_specs=[row(d), row(d), per_b(1, d), per_b(CONV_W - 1, d_ff), full(wup), full(cw), full(cb), full(wdn),
                  full(fg)],
        out_specs=[row(d), per_b(CONV_W - 1, d_ff)],
        out_shape=[jax.ShapeDtypeStruct((m, d), F32), jax.ShapeDtypeStruct((batch, CONV_W - 1, d_ff), F32)],
        scratch_shapes=[pltpu.VMEM((tm, d_ff), BF16), pltpu.VMEM((CONV_W - 1, d_ff), F32)],
        compiler_params=_cparams(("arbitrary", "arbitrary")),
        name="conv_ffn",
    )(h2, x1, g2, conv_prev, wup, cw, cb, wdn, fg)


def _ffn_short(h2, x1, g2_rows, conv_prev, wup, cw, cb, wdn, fg, *, batch, seq_len):
    m, d = x1.shape
    d_ff = cw.shape[1]
    assert CONV_W == 3 and seq_len >= CONV_W - 1
    zeros = lambda n: jnp.zeros((batch, n, d_ff), F32)
    p1 = jnp.concatenate([conv_prev[:, 1:2], zeros(seq_len - 1)], axis=1).reshape(m, d_ff)
    p2 = jnp.concatenate([conv_prev[:, 0:2], zeros(seq_len - 2)], axis=1).reshape(m, d_ff)
    tm = m
    row = lambda width: pl.BlockSpec((tm, width), lambda i: (i, 0))
    full = lambda a: pl.BlockSpec(a.shape, lambda i: (0,) * a.ndim)
    y, u = pl.pallas_call(
        functools.partial(_ffn_short_kernel, seq_len=seq_len),
        grid=(m // tm,),
        in_specs=[row(d), row(d), pl.BlockSpec((None, tm, d), lambda i: (0, i, 0)), row(d_ff), row(d_ff),
                  full(wup), full(cw), full(cb), full(wdn), full(fg)],
        out_specs=[row(d), row(d_ff)],
        out_shape=[jax.ShapeDtypeStruct((m, d), F32), jax.ShapeDtypeStruct((m, d_ff), F32)],
        scratch_shapes=[pltpu.VMEM((tm, d_ff), BF16)],
        compiler_params=_cparams(("arbitrary",)),
        name="conv_ffn_short",
    )(h2, x1, g2_rows, p1, p2, wup, cw, cb, wdn, fg)
    new_conv = u.reshape(batch, seq_len, d_ff)[:, seq_len - (CONV_W - 1):, :]
    return y, new_conv


def kernel(x_prompt, x_sample, cache_k, cache_v, state_hgrn, state_conv, page_table, c_prompt, c_sample,
           norm1_g, norm2_g, w_ada, b_ada, w_in, hgrn_lb_logits, hg_norm_g, w_a_out, w_b_out, w_o,
           w_up, conv_w, conv_b, w_down, final_g):
    depth = w_in.shape[0]
    assert depth == 1 and hgrn_lb_logits.shape[0] == depth + 1
    bp, tp, d = x_prompt.shape
    bs, ts, _ = x_sample.shape
    d_ff = conv_w.shape[-1]
    w = A_WIDTH

    w_ada_bf = w_ada[0].astype(BF16)
    w_in_bf = w_in[0].astype(BF16)
    wa_bf = w_a_out[0].astype(BF16)
    wb_bf = w_b_out[0].astype(BF16)
    wo_bf = w_o[0].astype(BF16)
    wup_bf = w_up[0].astype(BF16)
    wdn_bf = w_down[0].astype(BF16)
    n1 = norm1_g[0].reshape(1, d)
    n2 = norm2_g[0].reshape(1, d)
    fg = final_g.reshape(1, d)
    hgn = hg_norm_g[0].reshape(1, HG_WIDTH)
    cw = conv_w[0]
    cb = conv_b[0].reshape(1, d_ff)

    mod = _ada_mod(jnp.concatenate([c_prompt, c_sample], axis=0), w_ada_bf, b_ada[0].reshape(1, 6 * d))

    mp = bp * tp
    modp = [mod[:bp, i * d:(i + 1) * d].reshape(bp, 1, d) for i in range(6)]
    xp = x_prompt.reshape(mp, d)
    (qt, kt32, vt32, kb, vtb, qh, lf, kh, hi, hg, ga, gb) = _in_proj(
        xp, modp[0], modp[1], n1, w_in_bf, hgrn_lb_logits, per_row=False, seq_len=tp)
    seq3 = lambda a: a.reshape(bp, tp, w)
    oa = _moba_prompt(qt, seq3(kb), vtb).reshape(mp, w)
    ob, sp = _hgrn(seq3(qh), seq3(lf), seq3(kh), seq3(hi), seq3(hg),
                   jnp.zeros((bp, HG_HEADS, HG_DK, HG_DV), F32), hgn)
    x1, h2 = _mix_out(oa, ob.reshape(mp, w), ga, gb, xp, modp[2], modp[3], modp[4], n2, wa_bf, wb_bf, wo_bf,
                      per_row=False, seq_len=tp)
    yp, cp = _ffn_seq(h2, x1, modp[5], jnp.zeros((bp, CONV_W - 1, d_ff), F32), wup_bf, cw, cb, wdn_bf, fg,
                      batch=bp, seq_len=tp)

    ms = bs * ts
    mods_rows = jnp.repeat(mod[bp:], ts, axis=0)
    mods = [mods_rows[:, i * d:(i + 1) * d].reshape(1, ms, d) for i in range(6)]
    xs = x_sample.reshape(ms, d)
    (q_s, k32_s, v32_s, kb_s, vb_s, qh_s, lf_s, kh_s, hi_s, hg_s, ga_s, gb_s) = _in_proj(
        xs, mods[0], mods[1], n1, w_in_bf, hgrn_lb_logits, per_row=True, seq_len=ts)
    seq3s = lambda a: a.reshape(bs, ts, w)
    n_phys = cache_k.shape[1]
    page_t = lambda c: jnp.transpose(c[0], (0, 2, 3, 1)).reshape(n_phys, w, PAGE_SIZE)
    oa_s = _moba_decode(seq3s(q_s), seq3s(kb_s), seq3s(vb_s), page_t(cache_k), page_t(cache_v), page_table)
    t_pad = HG_CHUNK * pl.cdiv(ts, HG_CHUNK)
    padt = lambda a: jnp.pad(seq3s(a), ((0, 0), (0, t_pad - ts), (0, 0)))
    ob_s, ss = _hgrn(padt(qh_s), padt(lf_s), padt(kh_s), padt(hi_s), padt(hg_s), state_hgrn[0], hgn)
    ob_s = ob_s[:, :ts, :].reshape(ms, w)
    x1_s, h2_s = _mix_out(oa_s.reshape(ms, w).astype(BF16), ob_s, ga_s, gb_s, xs, mods[2], mods[3], mods[4], n2,
                          wa_bf, wb_bf, wo_bf, per_row=True, seq_len=ts)
    ys, cs = _ffn_short(h2_s, x1_s, mods[5], state_conv[0], wup_bf, cw, cb, wdn_bf, fg, batch=bs, seq_len=ts)

    kv_p = lambda a: jnp.transpose(a.reshape(1, bp, A_HEADS, A_HD, tp), (0, 1, 4, 2, 3))
    kv_s = lambda a: a.reshape(1, bs, ts, A_HEADS, A_HD)
    return (yp.reshape(bp, tp, d), ys.reshape(bs, ts, d), kv_p(kt32), kv_p(vt32), kv_s(k32_s), kv_s(v32_s),
            sp[None], ss[None], cp[None], cs[None])
```

```python
import functools

import jax
import jax.numpy as jnp
from jax import lax
from jax.experimental import pallas as pl
from jax.experimental.pallas import tpu as pltpu

F32 = jnp.float32
BF16 = jnp.bfloat16

A_HEADS = 8
A_HD = 64
A_WIDTH = A_HEADS * A_HD
MOBA_BLOCK = 256
MOBA_TOPK = 3
GATE_GROUP = 16
VALUE_ROWS = A_HD + 16
HG_HEADS = 4
HG_DK = 128
HG_DV = 128
HG_WIDTH = HG_HEADS * HG_DV
HG_CHUNK = 64
CONV_W = 3
PAGE_SIZE = 128
EPS = 1e-6
NEG_INF = float("-inf")
MASK_BIAS = -1e30
LOG2E = 1.4426950408889634

VMEM_LIMIT_BYTES = 56 * 1024 * 1024
ROW_TILE = 512
FF_TILE = 256
FFN_ROW_TILE = 512
DECODE_PAGES_PER_STEP = 16
DECODE_RING = 4
SCORE_LOOKAHEAD = 6


def _cparams(semantics, flags=None):
    return pltpu.CompilerParams(dimension_semantics=semantics, vmem_limit_bytes=VMEM_LIMIT_BYTES, flags=flags)


def _nt(a, b):
    return lax.dot_general(a, b, (((1,), (1,)), ((), ())), preferred_element_type=F32)


def _tn(a, b):
    return lax.dot_general(a, b, (((0,), (0,)), ((), ())), preferred_element_type=F32)


def _mm(a, b):
    return jnp.dot(a, b, preferred_element_type=F32)


def _rms(x, g):
    return x * lax.rsqrt(jnp.mean(x * x, axis=-1, keepdims=True) + EPS) * g


def _ada_kernel(c_ref, w_ref, b_ref, o_ref):
    c = c_ref[...]
    s = c * jax.nn.sigmoid(c)
    o_ref[...] = _mm(s.astype(BF16), w_ref[...]) + b_ref[...]


def _ada_mod(c, w_bf, b):
    n, d = c.shape
    cols = w_bf.shape[1]
    tn = 1536
    assert cols % tn == 0
    return pl.pallas_call(
        _ada_kernel,
        grid=(cols // tn,),
        in_specs=[pl.BlockSpec((n, d), lambda j: (0, 0)),
                  pl.BlockSpec((d, tn), lambda j: (0, j)),
                  pl.BlockSpec((1, tn), lambda j: (0, j))],
        out_specs=pl.BlockSpec((n, tn), lambda j: (0, j)),
        out_shape=jax.ShapeDtypeStruct((n, cols), F32),
        compiler_params=_cparams(("arbitrary",)),
        name="ada_mod",
    )(c, w_bf, b)


def _mod_spec(per_row, tm, tiles_per_seq, d):
    if per_row:
        return pl.BlockSpec((None, tm, d), lambda i: (0, i, 0))
    return pl.BlockSpec((None, 1, d), lambda i: (i // tiles_per_seq, 0, 0))


def _inproj_kernel(x_ref, sh_ref, sc_ref, g_ref, w_ref, lbl_ref,
                   q_ref, k32_ref, v32_ref, kb_ref, vb_ref, qh_ref, lf_ref, kh_ref, hi_ref, hg_ref,
                   ga_ref, gb_ref, hb_ref, *, d_model, kv_transposed):
    hb_ref[...] = (_rms(x_ref[...], g_ref[...]) * (1.0 + sc_ref[...]) + sh_ref[...]).astype(BF16)
    w = A_WIDTH

    def ep_q(q):
        q = q * (A_HD ** -0.5 * LOG2E)
        q_ref[...] = (q.T if kv_transposed else q).astype(BF16)

    def ep_k(k):
        kb_ref[...] = k.astype(BF16)
        k32_ref[...] = k.T if kv_transposed else k

    def ep_v(v):
        v = v.T if kv_transposed else v
        v32_ref[...] = v
        vb_ref[...] = v.astype(BF16)

    def ep_hq(hq):
        qh_ref[...] = (hq * jax.nn.sigmoid(hq)).astype(BF16)

    def ep_hf(hf):
        lbl = lbl_ref[...]
        e = jnp.exp(lbl - jnp.max(lbl, axis=0, keepdims=True))
        lb = e[0:1, :] / jnp.sum(e, axis=0, keepdims=True)
        en = jnp.exp(-jnp.abs(hf))
        r = 1.0 / (1.0 + en)
        pos = hf >= 0.0
        sg = jnp.where(pos, r, en * r)
        sgn = jnp.where(pos, en * r, r)
        lf_ref[...] = jnp.log(lb + (1.0 - lb) * sg)
        kh_ref[...] = ((1.0 - lb) * sgn).astype(BF16)

    def ep_hi(hi):
        hi_ref[...] = hi.astype(BF16)

    def ep_hg(hg):
        hg_ref[...] = (hg * jax.nn.sigmoid(hg)).astype(BF16)

    def ep_ga(ga):
        ga_ref[...] = jax.nn.sigmoid(ga).astype(BF16)

    def ep_gb(gb):
        gb_ref[...] = jax.nn.sigmoid(gb).astype(BF16)

    groups = [(w, ep_q), (w, ep_k), (w, ep_v), (w, ep_hq), (w, ep_hf), (w, ep_hi), (w, ep_hg),
              (d_model, ep_ga), (d_model, ep_gb)]
    starts = [sum(width for width, _ in groups[:n]) for n in range(len(groups))]
    for n in (7, 8, 4, 3, 6, 0, 1, 2, 5):
        width, epilogue = groups[n]
        epilogue(_mm(hb_ref[...], w_ref[:, starts[n]:starts[n] + width]))


def _in_proj(x2d, sh1, sc1, norm_g, w_in_bf, lb_logits, *, per_row, seq_len):
    m, d = x2d.shape
    tm = min(ROW_TILE, m)
    assert m % tm == 0 and (per_row or seq_len % tm == 0)
    tps = max(seq_len // tm, 1)
    mod = _mod_spec(per_row, tm, tps, d)
    row = lambda width: pl.BlockSpec((tm, width), lambda i: (i, 0))
    full = lambda a: pl.BlockSpec(a.shape, lambda i: (0,) * a.ndim)
    w = A_WIDTH
    outs = [(w, BF16), (w, F32), (w, F32), (w, BF16), (w, BF16), (w, BF16), (w, F32), (w, BF16), (w, BF16),
            (w, BF16), (d, BF16), (d, BF16)]
    out_specs = [row(width) for width, _ in outs]
    out_shape = [jax.ShapeDtypeStruct((m, width), dt) for width, dt in outs]
    kv_transposed = not per_row
    if kv_transposed:
        for idx in (0, 1, 2, 4):
            out_specs[idx] = pl.BlockSpec((None, w, tm), lambda i: (i // tps, 0, i % tps))
            out_shape[idx] = jax.ShapeDtypeStruct((m // seq_len, w, seq_len), outs[idx][1])
    return pl.pallas_call(
        functools.partial(_inproj_kernel, d_model=d, kv_transposed=kv_transposed),
        grid=(m // tm,),
        in_specs=[row(d), mod, mod, full(norm_g), full(w_in_bf), full(lb_logits)],
        out_specs=out_specs,
        out_shape=out_shape,
        scratch_shapes=[pltpu.VMEM((tm, d), BF16)],
        compiler_params=_cparams(("arbitrary",)),
        name="in_proj",
    )(x2d, sh1, sc1, norm_g, w_in_bf, lb_logits)


def _moba_prompt_kernel(qt_ref, k_ref, vt_ref, o_ref, km_ref, kmbd_ref, kaug_ref, vaug_ref, qaug_ref, gate_ref, m_ref,
                        acc_ref, *, nb):
    i = pl.program_id(1)
    blk = MOBA_BLOCK
    gg = GATE_GROUP
    vr = VALUE_ROWS
    lane = lax.broadcasted_iota(jnp.int32, (blk, 128), 1)

    def head_lanes(x, h):
        part = x[:, (h // 2) * 128:(h // 2 + 1) * 128]
        return pltpu.roll(part, A_HD, axis=1) if h % 2 else part

    @pl.when(i == 0)
    def _():
        qaug_ref[...] = jnp.zeros_like(qaug_ref)
        km_ref[...] = jnp.zeros_like(km_ref)
        for n in range(nb):
            rs = slice(n * blk, (n + 1) * blk)
            kf = k_ref[rs, :].astype(F32)
            km_ref[n:n + 1, :] = jnp.mean(kf, axis=0, keepdims=True)
            onehot = jnp.where(lane == A_HD + n, 1.0, 0.0)
            for h in range(A_HEADS):
                kaug_ref[rs, h * 128:(h + 1) * 128] = jnp.where(lane < A_HD, head_lanes(kf, h), onehot).astype(BF16)
        km = km_ref[...]
        lane_head = lax.broadcasted_iota(jnp.int32, km.shape, 1) // A_HD
        ones_row = jnp.where(lax.broadcasted_iota(jnp.int32, (vr - A_HD, vt_ref.shape[1]), 0) == 0, 1.0, 0.0)
        for h in range(A_HEADS):
            kmbd_ref[h * gg:(h + 1) * gg, :] = jnp.where(lane_head == h, km, 0.0).astype(BF16)
            vaug_ref[h * vr:h * vr + A_HD, :] = vt_ref[h * A_HD:(h + 1) * A_HD, :]
            vaug_ref[h * vr + A_HD:(h + 1) * vr, :] = ones_row.astype(BF16)

    qt = qt_ref[...]
    gate_ref[...] = _mm(kmbd_ref[...], qt)
    n_id = lax.broadcasted_iota(jnp.int32, (gg, blk), 0)
    for h in range(A_HEADS):
        g_h = gate_ref[h * gg:(h + 1) * gg, :]
        rank = jnp.zeros(g_h.shape, jnp.int32)
        for m in range(nb):
            g_m = gate_ref[h * gg + m:h * gg + m + 1, :]
            beats = (g_m > g_h) | ((g_m == g_h) & (m < n_id))
            rank = rank + jnp.where(beats, jnp.where(m < i, 1, 0), 0)
        bias_h = jnp.where((n_id < i) & (rank >= MOBA_TOPK), MASK_BIAS, 0.0)
        qaug_ref[h, 0:A_HD, :] = qt[h * A_HD:(h + 1) * A_HD, :]
        qaug_ref[h, A_HD:A_HD + gg, :] = bias_h.astype(BF16)

    def scores(j0, h):
        return _mm(kaug_ref[pl.ds(j0, blk), h * 128:(h + 1) * 128], qaug_ref[h])

    own0 = pl.multiple_of(i * blk, blk)
    causal = (lax.broadcasted_iota(jnp.int32, (blk, blk), 0) <= lax.broadcasted_iota(jnp.int32, (blk, blk), 1))

    def own_scores(h):
        return jnp.where(causal, scores(own0, h), NEG_INF)

    def accumulate(j0, s, h, first):
        hr = slice(h * vr, (h + 1) * vr)
        m_blk = jnp.max(s, axis=0, keepdims=True)
        if first:
            m_new = m_blk
        else:
            m_old = m_ref[h, 0:1, :]
            m_new = jnp.maximum(m_old, m_blk)
            alpha = jnp.exp2(m_old - m_new)
        p = jnp.exp2(s - m_new)
        pv = _mm(vaug_ref[hr, pl.ds(j0, blk)], p.astype(BF16))
        m_ref[h, 0:1, :] = m_new
        acc_ref[hr, :] = pv if first else alpha * acc_ref[hr, :] + pv

    def run_blocks(starts, score_fn, first):
        units = [(j0, h) for j0 in starts for h in range(A_HEADS)]
        pending = [score_fn(*u) for u in units[:SCORE_LOOKAHEAD]]
        for n, (j0, h) in enumerate(units):
            if n + SCORE_LOOKAHEAD < len(units):
                pending.append(score_fn(*units[n + SCORE_LOOKAHEAD]))
            accumulate(j0, pending.pop(0), h, first)

    run_blocks([own0], lambda j0, h: own_scores(h), True)

    def pair_body(t, carry):
        j0 = pl.multiple_of(t * (2 * blk), 2 * blk)
        run_blocks([j0, j0 + blk], scores, False)
        return carry

    lax.fori_loop(0, i // 2, pair_body, 0)

    @pl.when(i % 2 == 1)
    def _():
        run_blocks([pl.multiple_of((i - 1) * blk, blk)], scores, False)

    out_t = jnp.concatenate(
        [acc_ref[h * vr:h * vr + A_HD, :] / acc_ref[h * vr + A_HD:h * vr + A_HD + 1, :] for h in range(A_HEADS)], axis=0)
    o_ref[...] = out_t.T.astype(BF16)


def _moba_prompt(qt, k, vt):
    b, t, w = k.shape
    assert t % MOBA_BLOCK == 0
    nb = t // MOBA_BLOCK
    assert nb <= GATE_GROUP
    return pl.pallas_call(
        functools.partial(_moba_prompt_kernel, nb=nb),
        grid=(b, nb),
        in_specs=[pl.BlockSpec((None, w, MOBA_BLOCK), lambda bi, i: (bi, 0, i)),
                  pl.BlockSpec((None, t, w), lambda bi, i: (bi, 0, 0)),
                  pl.BlockSpec((None, w, t), lambda bi, i: (bi, 0, 0))],
        out_specs=pl.BlockSpec((None, MOBA_BLOCK, w), lambda bi, i: (bi, i, 0)),
        out_shape=jax.ShapeDtypeStruct((b, t, w), BF16),
        scratch_shapes=[pltpu.VMEM((GATE_GROUP, w), F32), pltpu.VMEM((A_HEADS * GATE_GROUP, w), BF16),
                        pltpu.VMEM((t, A_HEADS * 128), BF16),
                        pltpu.VMEM((A_HEADS * VALUE_ROWS, t), BF16),
                        pltpu.VMEM((A_HEADS, 128, MOBA_BLOCK), BF16),
                        pltpu.VMEM((A_HEADS * GATE_GROUP, MOBA_BLOCK), F32),
                        pltpu.VMEM((A_HEADS, 8, MOBA_BLOCK), F32),
                        pltpu.VMEM((A_HEADS * VALUE_ROWS, MOBA_BLOCK), F32)],
        compiler_params=_cparams(("arbitrary", "arbitrary")),
        name="moba_prompt",
    )(qt, k, vt)


def _moba_decode_kernel(pt_ref, qbd_ref, kn_ref, vn_ref, ck_ref, cv_ref, o_ref,
                        s_ref, g_ref, m_ref, l_ref, acc_ref, page_buf, page_sem, *, n_steps, pp, n_blocks, tq):
    s_id = pl.program_id(1)
    steps_per_batch = 2 * n_steps
    total_steps = pl.num_programs(0) * steps_per_batch
    g_step = pl.program_id(0) * steps_per_batch + s_id
    ring = DECODE_RING

    def page_copy(src_ref, page, slot):
        return pltpu.make_async_copy(src_ref.at[page], page_buf.at[slot], page_sem.at[slot])

    def issue(t):
        b_t = t // steps_per_batch
        s_t = t % steps_per_batch
        base = (t % ring) * pp

        @pl.when(s_t < n_steps)
        def _():
            for pi in range(pp):
                page_copy(ck_ref, pt_ref[b_t, s_t * pp + pi], base + pi).start()

        @pl.when(s_t >= n_steps)
        def _():
            for pi in range(pp):
                page_copy(cv_ref, pt_ref[b_t, (s_t - n_steps) * pp + pi], base + pi).start()

    @pl.when(g_step == 0)
    def _():
        for t in range(ring - 1):
            @pl.when(t < total_steps)
            def _(t=t):
                issue(jnp.int32(t))

    @pl.when(g_step + (ring - 1) < total_steps)
    def _():
        issue(g_step + (ring - 1))

    slot0 = (g_step % ring) * pp
    for pi in range(pp):
        page_copy(ck_ref, 0, slot0 + pi).wait()

    rows = tq * A_HEADS
    pages_per_block = MOBA_BLOCK // PAGE_SIZE
    qbd = qbd_ref[...]
    lane = lax.broadcasted_iota(jnp.int32, (rows, 128), 1)

    @pl.when(s_id == 0)
    def _():
        g_ref[...] = jnp.zeros_like(g_ref)

    @pl.when(s_id < n_steps)
    def _():
        part = None
        for pi in range(pp):
            page = s_id * pp + pi
            sc = _mm(qbd, page_buf[slot0 + pi].astype(BF16))
            s_ref[:, pl.ds(pl.multiple_of(page * PAGE_SIZE, PAGE_SIZE), PAGE_SIZE)] = sc
            rs = jnp.sum(sc, axis=-1, keepdims=True)
            part = rs if pi % pages_per_block == 0 else part + rs
            if pi % pages_per_block == pages_per_block - 1:
                n = page // pages_per_block
                g_ref[...] = jnp.where(lane == n, part * (1.0 / MOBA_BLOCK), g_ref[...])

    @pl.when(s_id == n_steps - 1)
    def _():
        g = jnp.where(lane < n_blocks, g_ref[...], NEG_INF)
        sel = jnp.zeros(g.shape, F32)
        for _ in range(min(MOBA_TOPK, n_blocks)):
            mx = jnp.max(g, axis=-1, keepdims=True)
            idx = jnp.min(jnp.where(g == mx, lane, 128), axis=-1, keepdims=True)
            pick = lane == idx
            sel = jnp.where(pick, 1.0, sel)
            g = jnp.where(pick, NEG_INF, g)

        def mask_block(n, m):
            picked = jnp.max(jnp.where(lane == n, sel, 0.0), axis=-1, keepdims=True)
            b0 = pl.multiple_of(n * MOBA_BLOCK, MOBA_BLOCK)
            sb = jnp.where(picked > 0.0, s_ref[:, pl.ds(b0, MOBA_BLOCK)], NEG_INF)
            s_ref[:, pl.ds(b0, MOBA_BLOCK)] = sb
            return jnp.maximum(m, jnp.max(sb, axis=-1, keepdims=True))

        m_past = lax.fori_loop(0, n_blocks, mask_block, jnp.full((rows, 1), NEG_INF, F32),
                               unroll=next(u for u in (8, 4, 2, 1) if n_blocks % u == 0))
        sn = _nt(qbd, kn_ref[...])
        qi = lax.broadcasted_iota(jnp.int32, sn.shape, 0) // A_HEADS
        sn = jnp.where(lane <= qi, sn, NEG_INF)
        m = jnp.maximum(m_past, jnp.max(sn, axis=-1, keepdims=True))
        pn = jnp.exp2(sn - m)
        m_ref[...] = jnp.broadcast_to(m, m_ref.shape)
        l_ref[...] = jnp.broadcast_to(jnp.sum(pn, axis=-1, keepdims=True), l_ref.shape)
        acc_ref[...] = _mm(pn.astype(BF16), vn_ref[...])

    @pl.when(s_id >= n_steps)
    def _():
        m = m_ref[:, 0:1]
        lsum = jnp.zeros((rows, 1), F32)
        acc = jnp.zeros(acc_ref.shape, F32)
        for pi in range(pp):
            page = (s_id - n_steps) * pp + pi
            sc = s_ref[:, pl.ds(pl.multiple_of(page * PAGE_SIZE, PAGE_SIZE), PAGE_SIZE)]
            p = jnp.exp2(sc - m)
            lsum = lsum + jnp.sum(p, axis=-1, keepdims=True)
            acc = acc + _nt(p.astype(BF16), page_buf[slot0 + pi].astype(BF16))
        l_ref[...] = l_ref[...] + lsum
        acc_ref[...] = acc_ref[...] + acc

    @pl.when(s_id == 2 * n_steps - 1)
    def _():
        o = acc_ref[...] / l_ref[:, 0:1]
        r_head = lax.broadcasted_iota(jnp.int32, o.shape, 0) % A_HEADS
        l_head = lax.broadcasted_iota(jnp.int32, o.shape, 1) // A_HD
        o = jnp.where(r_head == l_head, o, 0.0)
        o_ref[...] = jnp.zeros_like(o_ref)
        o_ref[0:tq, :] = jnp.sum(o.reshape(tq, A_HEADS, o.shape[-1]), axis=1)


def _moba_decode(q, k_new, v_new, cache_kt, cache_vt, page_table):
    b, tq, w = q.shape
    n_pages = page_table.shape[1]
    past = n_pages * PAGE_SIZE
    assert past % MOBA_BLOCK == 0 and tq <= PAGE_SIZE and tq <= MOBA_BLOCK
    n_blocks = past // MOBA_BLOCK
    assert n_blocks <= 128
    pp = DECODE_PAGES_PER_STEP
    assert n_pages % pp == 0 and pp % (MOBA_BLOCK // PAGE_SIZE) == 0
    n_steps = n_pages // pp
    rows = tq * A_HEADS
    head_of_lane = jnp.arange(w, dtype=jnp.int32) // A_HD
    head_mask = (head_of_lane[None, :] == jnp.arange(A_HEADS, dtype=jnp.int32)[:, None])
    qbd = jnp.where(head_mask[None, None], q[:, :, None, :], jnp.zeros((), q.dtype)).reshape(b, rows, w)
    pad = ((0, 0), (0, PAGE_SIZE - tq), (0, 0))
    kn = jnp.pad(k_new, pad)
    vn = jnp.pad(v_new, pad)

    per_b = lambda r: pl.BlockSpec((None, r, w), lambda bi, s, pt: (bi, 0, 0))
    hbm = pl.BlockSpec(memory_space=pl.ANY)
    out_rows = 8 * pl.cdiv(tq, 8)
    n_slots = DECODE_RING * pp
    grid_spec = pltpu.PrefetchScalarGridSpec(
        num_scalar_prefetch=1,
        grid=(b, 2 * n_steps),
        in_specs=[per_b(rows), per_b(PAGE_SIZE), per_b(PAGE_SIZE), hbm, hbm],
        out_specs=per_b(out_rows),
        scratch_shapes=[pltpu.VMEM((rows, past), F32), pltpu.VMEM((rows, 128), F32),
                        pltpu.VMEM((rows, 128), F32), pltpu.VMEM((rows, 128), F32),
                        pltpu.VMEM((rows, w), F32),
                        pltpu.VMEM((n_slots, w, PAGE_SIZE), F32), pltpu.SemaphoreType.DMA((n_slots,))],
    )
    out = pl.pallas_call(
        functools.partial(_moba_decode_kernel, n_steps=n_steps, pp=pp, n_blocks=n_blocks, tq=tq),
        grid_spec=grid_spec,
        out_shape=jax.ShapeDtypeStruct((b, out_rows, w), F32),
        compiler_params=_cparams(("arbitrary", "arbitrary")),
        name="moba_decode",
    )(page_table, qbd, kn, vn, cache_kt, cache_vt)
    return out[:, :tq, :]


def _hgrn_kernel(qh_ref, lf_ref, kh_ref, v_ref, hg_ref, s0_ref, g_ref, o_ref, st_out_ref, st_ref, *, n_chunks):
    t = pl.program_id(1)
    c = HG_CHUNK

    @pl.when(t == 0)
    def _():
        for h in range(HG_HEADS):
            st_ref[h] = s0_ref[h].T

    row = lax.broadcasted_iota(jnp.int32, (c, HG_DK), 0)
    tril = lax.broadcasted_iota(jnp.int32, (c, c), 1) <= lax.broadcasted_iota(jnp.int32, (c, c), 0)
    for ci in range(n_chunks):
        rs = slice(ci * c, (ci + 1) * c)
        for h in range(HG_HEADS):
            ks = slice(h * HG_DK, (h + 1) * HG_DK)
            vs = slice(h * HG_DV, (h + 1) * HG_DV)
            b = lf_ref[rs, ks]
            sh = 1
            while sh < c:
                b = b + jnp.where(row >= sh, pltpu.roll(b, sh, axis=0), 0.0)
                sh *= 2
            ref = b[c // 2 - 1:c // 2, :]
            last = b[c - 1:c, :]
            qh = qh_ref[rs, ks].astype(F32)
            kh = kh_ref[rs, ks].astype(F32)
            v = v_ref[rs, vs]
            qe = (qh * jnp.exp(b - ref)).astype(BF16)
            ke = (kh * jnp.exp(ref - b)).astype(BF16)
            a = jnp.where(tril, _nt(qe, ke), 0.0)
            qs = (qh * jnp.exp(b)).astype(BF16)
            kd = (kh * jnp.exp(last - b)).astype(BF16)
            st = st_ref[h]
            o = _mm(a.astype(BF16), v) + _nt(qs, st.astype(BF16))
            st_ref[h] = st * jnp.exp(last) + _tn(v, kd)
            y = _rms(o, g_ref[:, vs]) * hg_ref[rs, vs].astype(F32)
            o_ref[rs, vs] = y.astype(BF16)

    @pl.when(t == pl.num_programs(1) - 1)
    def _():
        for h in range(HG_HEADS):
            st_out_ref[h] = st_ref[h].T


def _hgrn(qh, lf, kh, v, hg, s0, norm_g):
    b, t, w = qh.shape
    assert t % HG_CHUNK == 0
    rows = ROW_TILE if t % ROW_TILE == 0 else HG_CHUNK
    seq = pl.BlockSpec((None, rows, w), lambda bi, ti: (bi, ti, 0))
    state = pl.BlockSpec((None, HG_HEADS, HG_DK, HG_DV), lambda bi, ti: (bi, 0, 0, 0))
    return pl.pallas_call(
        functools.partial(_hgrn_kernel, n_chunks=rows // HG_CHUNK),
        grid=(b, t // rows),
        in_specs=[seq, seq, seq, seq, seq, state, pl.BlockSpec(norm_g.shape, lambda bi, ti: (0, 0))],
        out_specs=[seq, state],
        out_shape=[jax.ShapeDtypeStruct((b, t, w), BF16), jax.ShapeDtypeStruct(s0.shape, F32)],
        scratch_shapes=[pltpu.VMEM((HG_HEADS, HG_DV, HG_DK), F32)],
        compiler_params=_cparams(("arbitrary", "arbitrary")),
        name="hgrn",
    )(qh, lf, kh, v, hg, s0, norm_g)


def _mix_kernel(oa_ref, ob_ref, ga_ref, gb_ref, x_ref, g1_ref, sh2_ref, sc2_ref, n2_ref, wa_ref, wb_ref, wo_ref,
                x1_ref, h2_ref):
    ya = _mm(oa_ref[...], wa_ref[...])
    yb = _mm(ob_ref[...], wb_ref[...])
    mixed = ga_ref[...].astype(F32) * ya + gb_ref[...].astype(F32) * yb
    x1 = x_ref[...] + g1_ref[...] * _mm(mixed.astype(BF16), wo_ref[...])
    x1_ref[...] = x1
    h2_ref[...] = (_rms(x1, n2_ref[...]) * (1.0 + sc2_ref[...]) + sh2_ref[...]).astype(BF16)


def _mix_out(oa, ob, ga, gb, x2d, g1, sh2, sc2, norm2_g, wa, wb, wo, *, per_row, seq_len):
    m, d = x2d.shape
    tm = min(ROW_TILE, m)
    tps = max(seq_len // tm, 1)
    mod = _mod_spec(per_row, tm, tps, d)
    row = lambda width: pl.BlockSpec((tm, width), lambda i: (i, 0))
    full = lambda a: pl.BlockSpec(a.shape, lambda i: (0,) * a.ndim)
    return pl.pallas_call(
        _mix_kernel,
        grid=(m // tm,),
        in_specs=[row(A_WIDTH), row(HG_WIDTH), row(d), row(d), row(d), mod, mod, mod, full(norm2_g),
                  full(wa), full(wb), full(wo)],
        out_specs=[row(d), row(d)],
        out_shape=[jax.ShapeDtypeStruct((m, d), F32), jax.ShapeDtypeStruct((m, d), BF16)],
        compiler_params=_cparams(("arbitrary",)),
        name="mix_out",
    )(oa, ob, ga, gb, x2d, g1, sh2, sc2, norm2_g, wa, wb, wo)


def _gelu(x):
    return 0.5 * x * (1.0 + lax.erf(x * (2.0 ** -0.5)))


def _ffn_body(h2_ref, x1_ref, g2_ref, wup_ref, cw_ref, cb_ref, wdn_ref, fg_ref, y_ref, act_ref, prev_rows, emit_u):
    d_ff = cw_ref.shape[1]
    h2 = h2_ref[...]
    for ft in range(d_ff // FF_TILE):
        fs = slice(ft * FF_TILE, (ft + 1) * FF_TILE)
        u = _mm(h2, wup_ref[:, fs])
        v = _mm(h2, wup_ref[:, d_ff + ft * FF_TILE:d_ff + (ft + 1) * FF_TILE])
        um1, um2 = prev_rows(u, fs)
        uc = cb_ref[:, fs] + cw_ref[0:1, fs] * um2 + cw_ref[1:2, fs] * um1 + cw_ref[2:3, fs] * u
        act_ref[:, fs] = (_gelu(uc) * v).astype(BF16)
        emit_u(u, fs)
    x2 = x1_ref[...] + g2_ref[...] * _mm(act_ref[...], wdn_ref[...])
    y_ref[...] = _rms(x2, fg_ref[...])


def _ffn_seq_kernel(h2_ref, x1_ref, g2_ref, cp_ref, wup_ref, cw_ref, cb_ref, wdn_ref, fg_ref,
                    y_ref, nc_ref, act_ref, carry_ref):
    tm = h2_ref.shape[0]

    @pl.when(pl.program_id(1) == 0)
    def _():
        carry_ref[...] = cp_ref[...]

    row = lax.broadcasted_iota(jnp.int32, (tm, FF_TILE), 0)

    def prev_rows(u, fs):
        c0 = carry_ref[0:1, fs]
        c1 = carry_ref[1:2, fs]
        um1 = jnp.where(row == 0, c1, pltpu.roll(u, 1, axis=0))
        um2 = jnp.where(row == 0, c0, jnp.where(row == 1, c1, pltpu.roll(u, 2, axis=0)))
        return um1, um2

    def emit_u(u, fs):
        tail = u[tm - (CONV_W - 1):tm, :]
        carry_ref[:, fs] = tail
        nc_ref[:, fs] = tail

    _ffn_body(h2_ref, x1_ref, g2_ref, wup_ref, cw_ref, cb_ref, wdn_ref, fg_ref, y_ref, act_ref, prev_rows, emit_u)


def _ffn_short_kernel(h2_ref, x1_ref, g2_ref, p1_ref, p2_ref, wup_ref, cw_ref, cb_ref, wdn_ref, fg_ref,
                      y_ref, u_ref, act_ref, *, seq_len):
    tm = h2_ref.shape[0]
    tpos = lax.broadcasted_iota(jnp.int32, (tm, FF_TILE), 0) % seq_len

    def prev_rows(u, fs):
        um1 = jnp.where(tpos >= 1, pltpu.roll(u, 1, axis=0), p1_ref[:, fs])
        um2 = jnp.where(tpos >= 2, pltpu.roll(u, 2, axis=0), p2_ref[:, fs])
        return um1, um2

    def emit_u(u, fs):
        u_ref[:, fs] = u

    _ffn_body(h2_ref, x1_ref, g2_ref, wup_ref, cw_ref, cb_ref, wdn_ref, fg_ref, y_ref, act_ref, prev_rows, emit_u)


def _ffn_seq(h2, x1, g2, conv_prev, wup, cw, cb, wdn, fg, *, batch, seq_len):
    m, d = x1.shape
    d_ff = cw.shape[1]
    tm = min(FFN_ROW_TILE, seq_len)
    assert seq_len % tm == 0 and d_ff % FF_TILE == 0 and tm >= CONV_W - 1
    tps = seq_len // tm
    row = lambda width: pl.BlockSpec((tm, width), lambda bi, ti: (bi * tps + ti, 0))
    full = lambda a: pl.BlockSpec(a.shape, lambda bi, ti: (0,) * a.ndim)
    per_b = lambda r, width: pl.BlockSpec((None, r, width), lambda bi, ti: (bi, 0, 0))
    return pl.pallas_call(
        _ffn_seq_kernel,
        grid=(batch, tps),
        in_specs=[row(d), row(d), per_b(1, d), per_b(CONV_W - 1, d_ff), full(wup), full(cw), full(cb), full(wdn),
                  full(fg)],
        out_specs=[row(d), per_b(CONV_W - 1, d_ff)],
        out_shape=[jax.ShapeDtypeStruct((m, d), F32), jax.ShapeDtypeStruct((batch, CONV_W - 1, d_ff), F32)],
        scratch_shapes=[pltpu.VMEM((tm, d_ff), BF16), pltpu.VMEM((CONV_W - 1, d_ff), F32)],
        compiler_params=_cparams(("arbitrary", "arbitrary")),
        name="conv_ffn",
    )(h2, x1, g2, conv_prev, wup, cw, cb, wdn, fg)


def _ffn_short(h2, x1, g2_rows, conv_prev, wup, cw, cb, wdn, fg, *, batch, seq_len):
    m, d = x1.shape
    d_ff = cw.shape[1]
    assert CONV_W == 3 and seq_len >= CONV_W - 1
    zeros = lambda n: jnp.zeros((batch, n, d_ff), F32)
    p1 = jnp.concatenate([conv_prev[:, 1:2], zeros(seq_len - 1)], axis=1).reshape(m, d_ff)
    p2 = jnp.concatenate([conv_prev[:, 0:2], zeros(seq_len - 2)], axis=1).reshape(m, d_ff)
    tm = m
    row = lambda width: pl.BlockSpec((tm, width), lambda i: (i, 0))
    full = lambda a: pl.BlockSpec(a.shape, lambda i: (0,) * a.ndim)
    y, u = pl.pallas_call(
        functools.partial(_ffn_short_kernel, seq_len=seq_len),
        grid=(m // tm,),
        in_specs=[row(d), row(d), pl.BlockSpec((None, tm, d), lambda i: (0, i, 0)), row(d_ff), row(d_ff),
                  full(wup), full(cw), full(cb), full(wdn), full(fg)],
        out_specs=[row(d), row(d_ff)],
        out_shape=[jax.ShapeDtypeStruct((m, d), F32), jax.ShapeDtypeStruct((m, d_ff), F32)],
        scratch_shapes=[pltpu.VMEM((tm, d_ff), BF16)],
        compiler_params=_cparams(("arbitrary",)),
        name="conv_ffn_short",
    )(h2, x1, g2_rows, p1, p2, wup, cw, cb, wdn, fg)
    new_conv = u.reshape(batch, seq_len, d_ff)[:, seq_len - (CONV_W - 1):, :]
    return y, new_conv


def kernel(x_prompt, x_sample, cache_k, cache_v, state_hgrn, state_conv, page_table, c_prompt, c_sample,
           norm1_g, norm2_g, w_ada, b_ada, w_in, hgrn_lb_logits, hg_norm_g, w_a_out, w_b_out, w_o,
           w_up, conv_w, conv_b, w_down, final_g):
    depth = w_in.shape[0]
    assert depth == 1 and hgrn_lb_logits.shape[0] == depth + 1
    bp, tp, d = x_prompt.shape
    bs, ts, _ = x_sample.shape
    d_ff = conv_w.shape[-1]
    w = A_WIDTH

    w_ada_bf = w_ada[0].astype(BF16)
    w_in_bf = w_in[0].astype(BF16)
    wa_bf = w_a_out[0].astype(BF16)
    wb_bf = w_b_out[0].astype(BF16)
    wo_bf = w_o[0].astype(BF16)
    wup_bf = w_up[0].astype(BF16)
    wdn_bf = w_down[0].astype(BF16)
    n1 = norm1_g[0].reshape(1, d)
    n2 = norm2_g[0].reshape(1, d)
    fg = final_g.reshape(1, d)
    hgn = hg_norm_g[0].reshape(1, HG_WIDTH)
    cw = conv_w[0]
    cb = conv_b[0].reshape(1, d_ff)

    mod = _ada_mod(jnp.concatenate([c_prompt, c_sample], axis=0), w_ada_bf, b_ada[0].reshape(1, 6 * d))

    mp = bp * tp
    modp = [mod[:bp, i * d:(i + 1) * d].reshape(bp, 1, d) for i in range(6)]
    xp = x_prompt.reshape(mp, d)
    (qt, kt32, vt32, kb, vtb, qh, lf, kh, hi, hg, ga, gb) = _in_proj(
        xp, modp[0], modp[1], n1, w_in_bf, hgrn_lb_logits, per_row=False, seq_len=tp)
    seq3 = lambda a: a.reshape(bp, tp, w)
    oa = _moba_prompt(qt, seq3(kb), vtb).reshape(mp, w)
    ob, sp = _hgrn(seq3(qh), seq3(lf), seq3(kh), seq3(hi), seq3(hg),
                   jnp.zeros((bp, HG_HEADS, HG_DK, HG_DV), F32), hgn)
    x1, h2 = _mix_out(oa, ob.reshape(mp, w), ga, gb, xp, modp[2], modp[3], modp[4], n2, wa_bf, wb_bf, wo_bf,
                      per_row=False, seq_len=tp)
    yp, cp = _ffn_seq(h2, x1, modp[5], jnp.zeros((bp, CONV_W - 1, d_ff), F32), wup_bf, cw, cb, wdn_bf, fg,
                      batch=bp, seq_len=tp)

    ms = bs * ts
    mods_rows = jnp.repeat(mod[bp:], ts, axis=0)
    mods = [mods_rows[:, i * d:(i + 1) * d].reshape(1, ms, d) for i in range(6)]
    xs = x_sample.reshape(ms, d)
    (q_s, k32_s, v32_s, kb_s, vb_s, qh_s, lf_s, kh_s, hi_s, hg_s, ga_s, gb_s) = _in_proj(
        xs, mods[0], mods[1], n1, w_in_bf, hgrn_lb_logits, per_row=True, seq_len=ts)
    seq3s = lambda a: a.reshape(bs, ts, w)
    n_phys = cache_k.shape[1]
    page_t = lambda c: jnp.transpose(c[0], (0, 2, 3, 1)).reshape(n_phys, w, PAGE_SIZE)
    oa_s = _moba_decode(seq3s(q_s), seq3s(kb_s), seq3s(vb_s), page_t(cache_k), page_t(cache_v), page_table)
    t_pad = HG_CHUNK * pl.cdiv(ts, HG_CHUNK)
    padt = lambda a: jnp.pad(seq3s(a), ((0, 0), (0, t_pad - ts), (0, 0)))
    ob_s, ss = _hgrn(padt(qh_s), padt(lf_s), padt(kh_s), padt(hi_s), padt(hg_s), state_hgrn[0], hgn)
    ob_s = ob_s[:, :ts, :].reshape(ms, w)
    x1_s, h2_s = _mix_out(oa_s.reshape(ms, w).astype(BF16), ob_s, ga_s, gb_s, xs, mods[2], mods[3], mods[4], n2,
                          wa_bf, wb_bf, wo_bf, per_row=True, seq_len=ts)
    ys, cs = _ffn_short(h2_s, x1_s, mods[5], state_conv[0], wup_bf, cw, cb, wdn_bf, fg, batch=bs, seq_len=ts)

    kv_p = lambda a: jnp.transpose(a.reshape(1, bp, A_HEADS, A_HD, tp), (0, 1, 4, 2, 3))
    kv_s = lambda a: a.reshape(1, bs, ts, A_HEADS, A_HD)
    return (yp.reshape(bp, tp, d), ys.reshape(bs, ts, d), kv_p(kt32), kv_p(vt32), kv_s(k32_s), kv_s(v32_s),
            sp[None], ss[None], cp[None], cs[None])
```

```python
import functools

import jax
import jax.numpy as jnp
from jax import lax
from jax.experimental import pallas as pl
from jax.experimental.pallas import tpu as pltpu

F32 = jnp.float32
BF16 = jnp.bfloat16

A_HEADS = 8
A_HD = 64
A_WIDTH = A_HEADS * A_HD
MOBA_BLOCK = 256
MOBA_TOPK = 3
GATE_GROUP = 16
VALUE_ROWS = A_HD + 16
HG_HEADS = 4
HG_DK = 128
HG_DV = 128
HG_WIDTH = HG_HEADS * HG_DV
HG_CHUNK = 64
CONV_W = 3
PAGE_SIZE = 128
EPS = 1e-6
NEG_INF = float("-inf")
MASK_BIAS = -1e30
LOG2E = 1.4426950408889634

VMEM_LIMIT_BYTES = 56 * 1024 * 1024
ROW_TILE = 512
FF_TILE = 256
FFN_ROW_TILE = 512
DECODE_PAGES_PER_STEP = 16
DECODE_RING = 4
MIX_SUBTILES = 2
HGRN_LOOKAHEAD = 3
assert HGRN_LOOKAHEAD < HG_HEADS
SCORE_LOOKAHEAD = 6


def _cparams(semantics, flags=None):
    return pltpu.CompilerParams(dimension_semantics=semantics, vmem_limit_bytes=VMEM_LIMIT_BYTES, flags=flags)


def _nt(a, b):
    return lax.dot_general(a, b, (((1,), (1,)), ((), ())), preferred_element_type=F32)


def _tn(a, b):
    return lax.dot_general(a, b, (((0,), (0,)), ((), ())), preferred_element_type=F32)


def _mm(a, b):
    return jnp.dot(a, b, preferred_element_type=F32)


def _rms(x, g):
    return x * lax.rsqrt(jnp.mean(x * x, axis=-1, keepdims=True) + EPS) * g


def _ada_kernel(c_ref, w_ref, b_ref, o_ref):
    c = c_ref[...]
    s = c * jax.nn.sigmoid(c)
    o_ref[...] = _mm(s.astype(BF16), w_ref[...]) + b_ref[...]


def _ada_mod(c, w_bf, b):
    n, d = c.shape
    cols = w_bf.shape[1]
    tn = 1536
    assert cols % tn == 0
    return pl.pallas_call(
        _ada_kernel,
        grid=(cols // tn,),
        in_specs=[pl.BlockSpec((n, d), lambda j: (0, 0)),
                  pl.BlockSpec((d, tn), lambda j: (0, j)),
                  pl.BlockSpec((1, tn), lambda j: (0, j))],
        out_specs=pl.BlockSpec((n, tn), lambda j: (0, j)),
        out_shape=jax.ShapeDtypeStruct((n, cols), F32),
        compiler_params=_cparams(("arbitrary",)),
        name="ada_mod",
    )(c, w_bf, b)


def _mod_spec(per_row, tm, tiles_per_seq, d):
    if per_row:
        return pl.BlockSpec((None, tm, d), lambda i: (0, i, 0))
    return pl.BlockSpec((None, 1, d), lambda i: (i // tiles_per_seq, 0, 0))


def _inproj_kernel(x_ref, sh_ref, sc_ref, g_ref, w_ref, lbl_ref,
                   q_ref, k32_ref, v32_ref, kb_ref, vb_ref, qh_ref, lf_ref, kh_ref, hi_ref, hg_ref,
                   ga_ref, gb_ref, hb_ref, *, d_model, kv_transposed):
    hb_ref[...] = (_rms(x_ref[...], g_ref[...]) * (1.0 + sc_ref[...]) + sh_ref[...]).astype(BF16)
    w = A_WIDTH

    def ep_q(q):
        q = q * (A_HD ** -0.5 * LOG2E)
        q_ref[...] = (q.T if kv_transposed else q).astype(BF16)

    def ep_k(k):
        kb_ref[...] = k.astype(BF16)
        k32_ref[...] = k.T if kv_transposed else k

    def ep_v(v):
        v = v.T if kv_transposed else v
        v32_ref[...] = v
        vb_ref[...] = v.astype(BF16)

    def ep_hq(hq):
        qh_ref[...] = (hq * jax.nn.sigmoid(hq)).astype(BF16)

    def ep_hf(hf):
        lbl = lbl_ref[...]
        e = jnp.exp(lbl - jnp.max(lbl, axis=0, keepdims=True))
        lb = e[0:1, :] / jnp.sum(e, axis=0, keepdims=True)
        en = jnp.exp(-jnp.abs(hf))
        r = 1.0 / (1.0 + en)
        pos = hf >= 0.0
        sg = jnp.where(pos, r, en * r)
        sgn = jnp.where(pos, en * r, r)
        lf_ref[...] = jnp.log(lb + (1.0 - lb) * sg)
        kh_ref[...] = ((1.0 - lb) * sgn).astype(BF16)

    def ep_hi(hi):
        hi_ref[...] = hi.astype(BF16)

    def ep_hg(hg):
        hg_ref[...] = (hg * jax.nn.sigmoid(hg)).astype(BF16)

    def ep_ga(ga):
        ga_ref[...] = jax.nn.sigmoid(ga).astype(BF16)

    def ep_gb(gb):
        gb_ref[...] = jax.nn.sigmoid(gb).astype(BF16)

    groups = [(w, ep_q), (w, ep_k), (w, ep_v), (w, ep_hq), (w, ep_hf), (w, ep_hi), (w, ep_hg),
              (d_model, ep_ga), (d_model, ep_gb)]
    starts = [sum(width for width, _ in groups[:n]) for n in range(len(groups))]
    for n in (7, 8, 4, 3, 6, 0, 1, 2, 5):
        width, epilogue = groups[n]
        epilogue(_mm(hb_ref[...], w_ref[:, starts[n]:starts[n] + width]))


def _in_proj(x2d, sh1, sc1, norm_g, w_in_bf, lb_logits, *, per_row, seq_len):
    m, d = x2d.shape
    tm = min(ROW_TILE, m)
    assert m % tm == 0 and (per_row or seq_len % tm == 0)
    tps = max(seq_len // tm, 1)
    mod = _mod_spec(per_row, tm, tps, d)
    row = lambda width: pl.BlockSpec((tm, width), lambda i: (i, 0))
    full = lambda a: pl.BlockSpec(a.shape, lambda i: (0,) * a.ndim)
    w = A_WIDTH
    outs = [(w, BF16), (w, F32), (w, F32), (w, BF16), (w, BF16), (w, BF16), (w, F32), (w, BF16), (w, BF16),
            (w, BF16), (d, BF16), (d, BF16)]
    out_specs = [row(width) for width, _ in outs]
    out_shape = [jax.ShapeDtypeStruct((m, width), dt) for width, dt in outs]
    kv_transposed = not per_row
    if kv_transposed:
        for idx in (0, 1, 2, 4):
            out_specs[idx] = pl.BlockSpec((None, w, tm), lambda i: (i // tps, 0, i % tps))
            out_shape[idx] = jax.ShapeDtypeStruct((m // seq_len, w, seq_len), outs[idx][1])
    return pl.pallas_call(
        functools.partial(_inproj_kernel, d_model=d, kv_transposed=kv_transposed),
        grid=(m // tm,),
        in_specs=[row(d), mod, mod, full(norm_g), full(w_in_bf), full(lb_logits)],
        out_specs=out_specs,
        out_shape=out_shape,
        scratch_shapes=[pltpu.VMEM((tm, d), BF16)],
        compiler_params=_cparams(("arbitrary",)),
        name="in_proj",
    )(x2d, sh1, sc1, norm_g, w_in_bf, lb_logits)


def _moba_prompt_kernel(qt_ref, k_ref, vt_ref, o_ref, km_ref, kmbd_ref, kaug_ref, vaug_ref, qaug_ref, gate_ref, m_ref,
                        acc_ref, *, nb):
    i = pl.program_id(1)
    blk = MOBA_BLOCK
    gg = GATE_GROUP
    vr = VALUE_ROWS
    lane = lax.broadcasted_iota(jnp.int32, (blk, 128), 1)

    def head_lanes(x, h):
        part = x[:, (h // 2) * 128:(h // 2 + 1) * 128]
        return pltpu.roll(part, A_HD, axis=1) if h % 2 else part

    @pl.when(i == 0)
    def _():
        qaug_ref[...] = jnp.zeros_like(qaug_ref)
        km_ref[...] = jnp.zeros_like(km_ref)
        for n in range(nb):
            rs = slice(n * blk, (n + 1) * blk)
            kf = k_ref[rs, :].astype(F32)
            km_ref[n:n + 1, :] = jnp.mean(kf, axis=0, keepdims=True)
            onehot = jnp.where(lane == A_HD + n, 1.0, 0.0)
            for h in range(A_HEADS):
                kaug_ref[rs, h * 128:(h + 1) * 128] = jnp.where(lane < A_HD, head_lanes(kf, h), onehot).astype(BF16)
        km = km_ref[...]
        lane_head = lax.broadcasted_iota(jnp.int32, km.shape, 1) // A_HD
        ones_row = jnp.where(lax.broadcasted_iota(jnp.int32, (vr - A_HD, vt_ref.shape[1]), 0) == 0, 1.0, 0.0)
        for h in range(A_HEADS):
            kmbd_ref[h * gg:(h + 1) * gg, :] = jnp.where(lane_head == h, km, 0.0).astype(BF16)
            vaug_ref[h * vr:h * vr + A_HD, :] = vt_ref[h * A_HD:(h + 1) * A_HD, :]
            vaug_ref[h * vr + A_HD:(h + 1) * vr, :] = ones_row.astype(BF16)

    qt = qt_ref[...]
    gate_ref[...] = _mm(kmbd_ref[...], qt)
    n_id = lax.broadcasted_iota(jnp.int32, (gg, blk), 0)
    for h in range(A_HEADS):
        g_h = gate_ref[h * gg:(h + 1) * gg, :]
        rank = jnp.zeros(g_h.shape, jnp.int32)
        for m in range(nb):
            g_m = gate_ref[h * gg + m:h * gg + m + 1, :]
            beats = (g_m > g_h) | ((g_m == g_h) & (m < n_id))
            rank = rank + jnp.where(beats, jnp.where(m < i, 1, 0), 0)
        bias_h = jnp.where((n_id < i) & (rank >= MOBA_TOPK), MASK_BIAS, 0.0)
        qaug_ref[h, 0:A_HD, :] = qt[h * A_HD:(h + 1) * A_HD, :]
        qaug_ref[h, A_HD:A_HD + gg, :] = bias_h.astype(BF16)

    def scores(j0, h):
        return _mm(kaug_ref[pl.ds(j0, blk), h * 128:(h + 1) * 128], qaug_ref[h])

    own0 = pl.multiple_of(i * blk, blk)
    causal = (lax.broadcasted_iota(jnp.int32, (blk, blk), 0) <= lax.broadcasted_iota(jnp.int32, (blk, blk), 1))

    def own_scores(h):
        return jnp.where(causal, scores(own0, h), NEG_INF)

    def accumulate(j0, s, h, first):
        hr = slice(h * vr, (h + 1) * vr)
        m_blk = jnp.max(s, axis=0, keepdims=True)
        if first:
            m_new = m_blk
        else:
            m_old = m_ref[h, 0:1, :]
            m_new = jnp.maximum(m_old, m_blk)
            alpha = jnp.exp2(m_old - m_new)
        p = jnp.exp2(s - m_new)
        pv = _mm(vaug_ref[hr, pl.ds(j0, blk)], p.astype(BF16))
        m_ref[h, 0:1, :] = m_new
        acc_ref[hr, :] = pv if first else alpha * acc_ref[hr, :] + pv

    def run_blocks(starts, score_fn, first):
        units = [(j0, h) for j0 in starts for h in range(A_HEADS)]
        pending = [score_fn(*u) for u in units[:SCORE_LOOKAHEAD]]
        for n, (j0, h) in enumerate(units):
            if n + SCORE_LOOKAHEAD < len(units):
                pending.append(score_fn(*units[n + SCORE_LOOKAHEAD]))
            accumulate(j0, pending.pop(0), h, first)

    run_blocks([own0], lambda j0, h: own_scores(h), True)

    def quad_body(t, carry):
        j0 = pl.multiple_of(t * (4 * blk), 4 * blk)
        run_blocks([j0 + n * blk for n in range(4)], scores, False)
        return carry

    lax.fori_loop(0, i // 4, quad_body, 0)

    @pl.when(i % 4 >= 2)
    def _():
        j0 = pl.multiple_of((i // 4) * (4 * blk), 2 * blk)
        run_blocks([j0, j0 + blk], scores, False)

    @pl.when(i % 2 == 1)
    def _():
        run_blocks([pl.multiple_of((i - 1) * blk, blk)], scores, False)

    out_t = jnp.concatenate(
        [acc_ref[h * vr:h * vr + A_HD, :] / acc_ref[h * vr + A_HD:h * vr + A_HD + 1, :] for h in range(A_HEADS)], axis=0)
    o_ref[...] = out_t.T.astype(BF16)


def _moba_prompt(qt, k, vt):
    b, t, w = k.shape
    assert t % MOBA_BLOCK == 0
    nb = t // MOBA_BLOCK
    assert nb <= GATE_GROUP
    return pl.pallas_call(
        functools.partial(_moba_prompt_kernel, nb=nb),
        grid=(b, nb),
        in_specs=[pl.BlockSpec((None, w, MOBA_BLOCK), lambda bi, i: (bi, 0, i)),
                  pl.BlockSpec((None, t, w), lambda bi, i: (bi, 0, 0)),
                  pl.BlockSpec((None, w, t), lambda bi, i: (bi, 0, 0))],
        out_specs=pl.BlockSpec((None, MOBA_BLOCK, w), lambda bi, i: (bi, i, 0)),
        out_shape=jax.ShapeDtypeStruct((b, t, w), BF16),
        scratch_shapes=[pltpu.VMEM((GATE_GROUP, w), F32), pltpu.VMEM((A_HEADS * GATE_GROUP, w), BF16),
                        pltpu.VMEM((t, A_HEADS * 128), BF16),
                        pltpu.VMEM((A_HEADS * VALUE_ROWS, t), BF16),
                        pltpu.VMEM((A_HEADS, 128, MOBA_BLOCK), BF16),
                        pltpu.VMEM((A_HEADS * GATE_GROUP, MOBA_BLOCK), F32),
                        pltpu.VMEM((A_HEADS, 8, MOBA_BLOCK), F32),
                        pltpu.VMEM((A_HEADS * VALUE_ROWS, MOBA_BLOCK), F32)],
        compiler_params=_cparams(("arbitrary", "arbitrary")),
        name="moba_prompt",
    )(qt, k, vt)


def _moba_decode_kernel(pt_ref, qbd_ref, kn_ref, vn_ref, ck_ref, cv_ref, o_ref,
                        s_ref, g_ref, m_ref, l_ref, acc_ref, page_buf, page_sem, *, n_steps, pp, n_blocks, tq):
    s_id = pl.program_id(1)
    steps_per_batch = 2 * n_steps
    total_steps = pl.num_programs(0) * steps_per_batch
    g_step = pl.program_id(0) * steps_per_batch + s_id
    ring = DECODE_RING

    def page_copy(src_ref, page, slot):
        return pltpu.make_async_copy(src_ref.at[page], page_buf.at[slot], page_sem.at[slot])

    def issue(t):
        b_t = t // steps_per_batch
        s_t = t % steps_per_batch
        base = (t % ring) * pp

        @pl.when(s_t < n_steps)
        def _():
            for pi in range(pp):
                page_copy(ck_ref, pt_ref[b_t, s_t * pp + pi], base + pi).start()

        @pl.when(s_t >= n_steps)
        def _():
            for pi in range(pp):
                page_copy(cv_ref, pt_ref[b_t, (s_t - n_steps) * pp + pi], base + pi).start()

    @pl.when(g_step == 0)
    def _():
        for t in range(ring - 1):
            @pl.when(t < total_steps)
            def _(t=t):
                issue(jnp.int32(t))

    @pl.when(g_step + (ring - 1) < total_steps)
    def _():
        issue(g_step + (ring - 1))

    slot0 = (g_step % ring) * pp
    for pi in range(pp):
        page_copy(ck_ref, 0, slot0 + pi).wait()

    rows = tq * A_HEADS
    pages_per_block = MOBA_BLOCK // PAGE_SIZE
    qbd = qbd_ref[...]
    lane = lax.broadcasted_iota(jnp.int32, (rows, 128), 1)

    @pl.when(s_id == 0)
    def _():
        g_ref[...] = jnp.zeros_like(g_ref)

    @pl.when(s_id < n_steps)
    def _():
        part = None
        for pi in range(pp):
            page = s_id * pp + pi
            sc = _mm(qbd, page_buf[slot0 + pi].astype(BF16))
            s_ref[:, pl.ds(pl.multiple_of(page * PAGE_SIZE, PAGE_SIZE), PAGE_SIZE)] = sc
            rs = jnp.sum(sc, axis=-1, keepdims=True)
            part = rs if pi % pages_per_block == 0 else part + rs
            if pi % pages_per_block == pages_per_block - 1:
                n = page // pages_per_block
                g_ref[...] = jnp.where(lane == n, part * (1.0 / MOBA_BLOCK), g_ref[...])

    @pl.when(s_id == n_steps - 1)
    def _():
        g = jnp.where(lane < n_blocks, g_ref[...], NEG_INF)
        sel = jnp.zeros(g.shape, F32)
        for _ in range(min(MOBA_TOPK, n_blocks)):
            mx = jnp.max(g, axis=-1, keepdims=True)
            idx = jnp.min(jnp.where(g == mx, lane, 128), axis=-1, keepdims=True)
            pick = lane == idx
            sel = jnp.where(pick, 1.0, sel)
            g = jnp.where(pick, NEG_INF, g)

        def mask_block(n, m):
            picked = jnp.max(jnp.where(lane == n, sel, 0.0), axis=-1, keepdims=True)
            b0 = pl.multiple_of(n * MOBA_BLOCK, MOBA_BLOCK)
            sb = jnp.where(picked > 0.0, s_ref[:, pl.ds(b0, MOBA_BLOCK)], NEG_INF)
            s_ref[:, pl.ds(b0, MOBA_BLOCK)] = sb
            return jnp.maximum(m, jnp.max(sb, axis=-1, keepdims=True))

        m_past = lax.fori_loop(0, n_blocks, mask_block, jnp.full((rows, 1), NEG_INF, F32),
                               unroll=next(u for u in (8, 4, 2, 1) if n_blocks % u == 0))
        sn = _nt(qbd, kn_ref[...])
        qi = lax.broadcasted_iota(jnp.int32, sn.shape, 0) // A_HEADS
        sn = jnp.where(lane <= qi, sn, NEG_INF)
        m = jnp.maximum(m_past, jnp.max(sn, axis=-1, keepdims=True))
        pn = jnp.exp2(sn - m)
        m_ref[...] = jnp.broadcast_to(m, m_ref.shape)
        l_ref[...] = jnp.broadcast_to(jnp.sum(pn, axis=-1, keepdims=True), l_ref.shape)
        acc_ref[...] = _mm(pn.astype(BF16), vn_ref[...])

    @pl.when(s_id >= n_steps)
    def _():
        m = m_ref[:, 0:1]
        lsum = jnp.zeros((rows, 1), F32)
        acc = jnp.zeros(acc_ref.shape, F32)
        for pi in range(pp):
            page = (s_id - n_steps) * pp + pi
            sc = s_ref[:, pl.ds(pl.multiple_of(page * PAGE_SIZE, PAGE_SIZE), PAGE_SIZE)]
            p = jnp.exp2(sc - m)
            lsum = lsum + jnp.sum(p, axis=-1, keepdims=True)
            acc = acc + _nt(p.astype(BF16), page_buf[slot0 + pi].astype(BF16))
        l_ref[...] = l_ref[...] + lsum
        acc_ref[...] = acc_ref[...] + acc

    @pl.when(s_id == 2 * n_steps - 1)
    def _():
        o = acc_ref[...] / l_ref[:, 0:1]
        r_head = lax.broadcasted_iota(jnp.int32, o.shape, 0) % A_HEADS
        l_head = lax.broadcasted_iota(jnp.int32, o.shape, 1) // A_HD
        o = jnp.where(r_head == l_head, o, 0.0)
        o_ref[...] = jnp.zeros_like(o_ref)
        o_ref[0:tq, :] = jnp.sum(o.reshape(tq, A_HEADS, o.shape[-1]), axis=1)


def _moba_decode(q, k_new, v_new, cache_kt, cache_vt, page_table):
    b, tq, w = q.shape
    n_pages = page_table.shape[1]
    past = n_pages * PAGE_SIZE
    assert past % MOBA_BLOCK == 0 and tq <= PAGE_SIZE and tq <= MOBA_BLOCK
    n_blocks = past // MOBA_BLOCK
    assert n_blocks <= 128
    pp = DECODE_PAGES_PER_STEP
    assert n_pages % pp == 0 and pp % (MOBA_BLOCK // PAGE_SIZE) == 0
    n_steps = n_pages // pp
    rows = tq * A_HEADS
    head_of_lane = jnp.arange(w, dtype=jnp.int32) // A_HD
    head_mask = (head_of_lane[None, :] == jnp.arange(A_HEADS, dtype=jnp.int32)[:, None])
    qbd = jnp.where(head_mask[None, None], q[:, :, None, :], jnp.zeros((), q.dtype)).reshape(b, rows, w)
    pad = ((0, 0), (0, PAGE_SIZE - tq), (0, 0))
    kn = jnp.pad(k_new, pad)
    vn = jnp.pad(v_new, pad)

    per_b = lambda r: pl.BlockSpec((None, r, w), lambda bi, s, pt: (bi, 0, 0))
    hbm = pl.BlockSpec(memory_space=pl.ANY)
    out_rows = 8 * pl.cdiv(tq, 8)
    n_slots = DECODE_RING * pp
    grid_spec = pltpu.PrefetchScalarGridSpec(
        num_scalar_prefetch=1,
        grid=(b, 2 * n_steps),
        in_specs=[per_b(rows), per_b(PAGE_SIZE), per_b(PAGE_SIZE), hbm, hbm],
        out_specs=per_b(out_rows),
        scratch_shapes=[pltpu.VMEM((rows, past), F32), pltpu.VMEM((rows, 128), F32),
                        pltpu.VMEM((rows, 128), F32), pltpu.VMEM((rows, 128), F32),
                        pltpu.VMEM((rows, w), F32),
                        pltpu.VMEM((n_slots, w, PAGE_SIZE), F32), pltpu.SemaphoreType.DMA((n_slots,))],
    )
    out = pl.pallas_call(
        functools.partial(_moba_decode_kernel, n_steps=n_steps, pp=pp, n_blocks=n_blocks, tq=tq),
        grid_spec=grid_spec,
        out_shape=jax.ShapeDtypeStruct((b, out_rows, w), F32),
        compiler_params=_cparams(("arbitrary", "arbitrary")),
        name="moba_decode",
    )(page_table, qbd, kn, vn, cache_kt, cache_vt)
    return out[:, :tq, :]


def _hgrn_kernel(qh_ref, lf_ref, kh_ref, v_ref, hg_ref, s0_ref, g_ref, o_ref, st_out_ref, st_ref, *, n_chunks):
    t = pl.program_id(1)
    c = HG_CHUNK

    @pl.when(t == 0)
    def _():
        for h in range(HG_HEADS):
            st_ref[h] = s0_ref[h].T

    row = lax.broadcasted_iota(jnp.int32, (c, HG_DK), 0)
    tril = lax.broadcasted_iota(jnp.int32, (c, c), 1) <= lax.broadcasted_iota(jnp.int32, (c, c), 0)

    def front(ci, h):
        rs = slice(ci * c, (ci + 1) * c)
        ks = slice(h * HG_DK, (h + 1) * HG_DK)
        vs = slice(h * HG_DV, (h + 1) * HG_DV)
        b = lf_ref[rs, ks]
        sh = 1
        while sh < c:
            b = b + jnp.where(row >= sh, pltpu.roll(b, sh, axis=0), 0.0)
            sh *= 2
        ref = b[c // 2 - 1:c // 2, :]
        last = b[c - 1:c, :]
        qh = qh_ref[rs, ks].astype(F32)
        kh = kh_ref[rs, ks].astype(F32)
        v = v_ref[rs, vs]
        qe = (qh * jnp.exp(b - ref)).astype(BF16)
        ke = (kh * jnp.exp(ref - b)).astype(BF16)
        qs = (qh * jnp.exp(b)).astype(BF16)
        kd = (kh * jnp.exp(last - b)).astype(BF16)
        st = st_ref[h]
        return dict(a=_nt(qe, ke), o_state=_nt(qs, st.astype(BF16)), st=st, upd=_tn(v, kd), decay=jnp.exp(last), v=v)

    def back(ci, h, f):
        rs = slice(ci * c, (ci + 1) * c)
        vs = slice(h * HG_DV, (h + 1) * HG_DV)
        a = jnp.where(tril, f["a"], 0.0).astype(BF16)
        o = _mm(a, f["v"]) + f["o_state"]
        st_ref[h] = f["st"] * f["decay"] + f["upd"]
        y = _rms(o, g_ref[:, vs]) * hg_ref[rs, vs].astype(F32)
        o_ref[rs, vs] = y.astype(BF16)

    units = [(ci, h) for ci in range(n_chunks) for h in range(HG_HEADS)]
    pending = [front(*u) for u in units[:HGRN_LOOKAHEAD]]
    for n, u in enumerate(units):
        if n + HGRN_LOOKAHEAD < len(units):
            pending.append(front(*units[n + HGRN_LOOKAHEAD]))
        back(*u, pending.pop(0))

    @pl.when(t == pl.num_programs(1) - 1)
    def _():
        for h in range(HG_HEADS):
            st_out_ref[h] = st_ref[h].T


def _hgrn(qh, lf, kh, v, hg, s0, norm_g):
    b, t, w = qh.shape
    assert t % HG_CHUNK == 0
    rows = ROW_TILE if t % ROW_TILE == 0 else HG_CHUNK
    seq = pl.BlockSpec((None, rows, w), lambda bi, ti: (bi, ti, 0))
    state = pl.BlockSpec((None, HG_HEADS, HG_DK, HG_DV), lambda bi, ti: (bi, 0, 0, 0))
    return pl.pallas_call(
        functools.partial(_hgrn_kernel, n_chunks=rows // HG_CHUNK),
        grid=(b, t // rows),
        in_specs=[seq, seq, seq, seq, seq, state, pl.BlockSpec(norm_g.shape, lambda bi, ti: (0, 0))],
        out_specs=[seq, state],
        out_shape=[jax.ShapeDtypeStruct((b, t, w), BF16), jax.ShapeDtypeStruct(s0.shape, F32)],
        scratch_shapes=[pltpu.VMEM((HG_HEADS, HG_DV, HG_DK), F32)],
        compiler_params=_cparams(("arbitrary", "arbitrary")),
        name="hgrn",
    )(qh, lf, kh, v, hg, s0, norm_g)


def _mix_kernel(oa_ref, ob_ref, ga_ref, gb_ref, x_ref, g1_ref, sh2_ref, sc2_ref, n2_ref, wa_ref, wb_ref, wo_ref,
                x1_ref, h2_ref):
    tm = x_ref.shape[0]
    n_sub = MIX_SUBTILES if tm % (8 * MIX_SUBTILES) == 0 else 1
    subs = [slice(n * (tm // n_sub), (n + 1) * (tm // n_sub)) for n in range(n_sub)]

    def mod_rows(ref, rs):
        return ref[...] if ref.shape[0] == 1 else ref[rs, :]

    branch = [(_mm(oa_ref[rs, :], wa_ref[...]), _mm(ob_ref[rs, :], wb_ref[...])) for rs in subs]
    merged = []
    for rs, (ya, yb) in zip(subs, branch):
        mixed = ga_ref[rs, :].astype(F32) * ya + gb_ref[rs, :].astype(F32) * yb
        merged.append(_mm(mixed.astype(BF16), wo_ref[...]))
    for rs, y in zip(subs, merged):
        x1 = x_ref[rs, :] + mod_rows(g1_ref, rs) * y
        x1_ref[rs, :] = x1
        h2_ref[rs, :] = (_rms(x1, n2_ref[...]) * (1.0 + mod_rows(sc2_ref, rs)) + mod_rows(sh2_ref, rs)).astype(BF16)


def _mix_out(oa, ob, ga, gb, x2d, g1, sh2, sc2, norm2_g, wa, wb, wo, *, per_row, seq_len):
    m, d = x2d.shape
    tm = min(ROW_TILE, m)
    tps = max(seq_len // tm, 1)
    mod = _mod_spec(per_row, tm, tps, d)
    row = lambda width: pl.BlockSpec((tm, width), lambda i: (i, 0))
    full = lambda a: pl.BlockSpec(a.shape, lambda i: (0,) * a.ndim)
    return pl.pallas_call(
        _mix_kernel,
        grid=(m // tm,),
        in_specs=[row(A_WIDTH), row(HG_WIDTH), row(d), row(d), row(d), mod, mod, mod, full(norm2_g),
                  full(wa), full(wb), full(wo)],
        out_specs=[row(d), row(d)],
        out_shape=[jax.ShapeDtypeStruct((m, d), F32), jax.ShapeDtypeStruct((m, d), BF16)],
        compiler_params=_cparams(("arbitrary",)),
        name="mix_out",
    )(oa, ob, ga, gb, x2d, g1, sh2, sc2, norm2_g, wa, wb, wo)


def _gelu(x):
    return 0.5 * x * (1.0 + lax.erf(x * (2.0 ** -0.5)))


def _ffn_body(h2_ref, x1_ref, g2_ref, wup_ref, cw_ref, cb_ref, wdn_ref, fg_ref, y_ref, act_ref, prev_rows, emit_u):
    d_ff = cw_ref.shape[1]
    h2 = h2_ref[...]
    for ft in range(d_ff // FF_TILE):
        fs = slice(ft * FF_TILE, (ft + 1) * FF_TILE)
        u = _mm(h2, wup_ref[:, fs])
        v = _mm(h2, wup_ref[:, d_ff + ft * FF_TILE:d_ff + (ft + 1) * FF_TILE])
        um1, um2 = prev_rows(u, fs)
        uc = cb_ref[:, fs] + cw_ref[0:1, fs] * um2 + cw_ref[1:2, fs] * um1 + cw_ref[2:3, fs] * u
        act_ref[:, fs] = (_gelu(uc) * v).astype(BF16)
        emit_u(u, fs)
    x2 = x1_ref[...] + g2_ref[...] * _mm(act_ref[...], wdn_ref[...])
    y_ref[...] = _rms(x2, fg_ref[...])


def _ffn_seq_kernel(h2_ref, x1_ref, g2_ref, cp_ref, wup_ref, cw_ref, cb_ref, wdn_ref, fg_ref,
                    y_ref, nc_ref, act_ref, carry_ref):
    tm = h2_ref.shape[0]

    @pl.when(pl.program_id(1) == 0)
    def _():
        carry_ref[...] = cp_ref[...]

    row = lax.broadcasted_iota(jnp.int32, (tm, FF_TILE), 0)

    def prev_rows(u, fs):
        c0 = carry_ref[0:1, fs]
        c1 = carry_ref[1:2, fs]
        um1 = jnp.where(row == 0, c1, pltpu.roll(u, 1, axis=0))
        um2 = jnp.where(row == 0, c0, jnp.where(row == 1, c1, pltpu.roll(u, 2, axis=0)))
        return um1, um2

    def emit_u(u, fs):
        tail = u[tm - (CONV_W - 1):tm, :]
        carry_ref[:, fs] = tail
        nc_ref[:, fs] = tail

    _ffn_body(h2_ref, x1_ref, g2_ref, wup_ref, cw_ref, cb_ref, wdn_ref, fg_ref, y_ref, act_ref, prev_rows, emit_u)


def _ffn_short_kernel(h2_ref, x1_ref, g2_ref, p1_ref, p2_ref, wup_ref, cw_ref, cb_ref, wdn_ref, fg_ref,
                      y_ref, u_ref, act_ref, *, seq_len):
    tm = h2_ref.shape[0]
    tpos = lax.broadcasted_iota(jnp.int32, (tm, FF_TILE), 0) % seq_len

    def prev_rows(u, fs):
        um1 = jnp.where(tpos >= 1, pltpu.roll(u, 1, axis=0), p1_ref[:, fs])
        um2 = jnp.where(tpos >= 2, pltpu.roll(u, 2, axis=0), p2_ref[:, fs])
        return um1, um2

    def emit_u(u, fs):
        u_ref[:, fs] = u

    _ffn_body(h2_ref, x1_ref, g2_ref, wup_ref, cw_ref, cb_ref, wdn_ref, fg_ref, y_ref, act_ref, prev_rows, emit_u)


def _ffn_seq(h2, x1, g2, conv_prev, wup, cw, cb, wdn, fg, *, batch, seq_len):
    m, d = x1.shape
    d_ff = cw.shape[1]
    tm = min(FFN_ROW_TILE, seq_len)
    assert seq_len % tm == 0 and d_ff % FF_TILE == 0 and tm >= CONV_W - 1
    tps = seq_len // tm
    row = lambda width: pl.BlockSpec((tm, width), lambda bi, ti: (bi * tps + ti, 0))
    full = lambda a: pl.BlockSpec(a.shape, lambda bi, ti: (0,) * a.ndim)
    per_b = lambda r, width: pl.BlockSpec((None, r, width), lambda bi, ti: (bi, 0, 0))
    return pl.pallas_call(
        _ffn_seq_kernel,
        grid=(batch, tps),
        in_specs=[row(d), row(d), per_b(1, d), per_b(CONV_W - 1, d_ff), full(wup), full(cw), full(cb), full(wdn),
                  full(fg)],
        out_specs=[row(d), per_b(CONV_W - 1, d_ff)],
        out_shape=[jax.ShapeDtypeStruct((m, d), F32), jax.ShapeDtypeStruct((batch, CONV_W - 1, d_ff), F32)],
        scratch_shapes=[pltpu.VMEM((tm, d_ff), BF16), pltpu.VMEM((CONV_W - 1, d_ff), F32)],
        compiler_params=_cparams(("arbitrary", "arbitrary")),
        name="conv_ffn",
    )(h2, x1, g2, conv_prev, wup, cw, cb, wdn, fg)


def _ffn_short(h2, x1, g2_rows, conv_prev, wup, cw, cb, wdn, fg, *, batch, seq_len):
    m, d = x1.shape
    d_ff = cw.shape[1]
    assert CONV_W == 3 and seq_len >= CONV_W - 1
    zeros = lambda n: jnp.zeros((batch, n, d_ff), F32)
    p1 = jnp.concatenate([conv_prev[:, 1:2], zeros(seq_len - 1)], axis=1).reshape(m, d_ff)
    p2 = jnp.concatenate([conv_prev[:, 0:2], zeros(seq_len - 2)], axis=1).reshape(m, d_ff)
    tm = m
    row = lambda width: pl.BlockSpec((tm, width), lambda i: (i, 0))
    full = lambda a: pl.BlockSpec(a.shape, lambda i: (0,) * a.ndim)
    y, u = pl.pallas_call(
        functools.partial(_ffn_short_kernel, seq_len=seq_len),
        grid=(m // tm,),
        in_specs=[row(d), row(d), pl.BlockSpec((None, tm, d), lambda i: (0, i, 0)), row(d_ff), row(d_ff),
                  full(wup), full(cw), full(cb), full(wdn), full(fg)],
        out_specs=[row(d), row(d_ff)],
        out_shape=[jax.ShapeDtypeStruct((m, d), F32), jax.ShapeDtypeStruct((m, d_ff), F32)],
        scratch_shapes=[pltpu.VMEM((tm, d_ff), BF16)],
        compiler_params=_cparams(("arbitrary",)),
        name="conv_ffn_short",
    )(h2, x1, g2_rows, p1, p2, wup, cw, cb, wdn, fg)
    new_conv = u.reshape(batch, seq_len, d_ff)[:, seq_len - (CONV_W - 1):, :]
    return y, new_conv


def kernel(x_prompt, x_sample, cache_k, cache_v, state_hgrn, state_conv, page_table, c_prompt, c_sample,
           norm1_g, norm2_g, w_ada, b_ada, w_in, hgrn_lb_logits, hg_norm_g, w_a_out, w_b_out, w_o,
           w_up, conv_w, conv_b, w_down, final_g):
    depth = w_in.shape[0]
    assert depth == 1 and hgrn_lb_logits.shape[0] == depth + 1
    bp, tp, d = x_prompt.shape
    bs, ts, _ = x_sample.shape
    d_ff = conv_w.shape[-1]
    w = A_WIDTH

    w_ada_bf = w_ada[0].astype(BF16)
    w_in_bf = w_in[0].astype(BF16)
    wa_bf = w_a_out[0].astype(BF16)
    wb_bf = w_b_out[0].astype(BF16)
    wo_bf = w_o[0].astype(BF16)
    wup_bf = w_up[0].astype(BF16)
    wdn_bf = w_down[0].astype(BF16)
    n1 = norm1_g[0].reshape(1, d)
    n2 = norm2_g[0].reshape(1, d)
    fg = final_g.reshape(1, d)
    hgn = hg_norm_g[0].reshape(1, HG_WIDTH)
    cw = conv_w[0]
    cb = conv_b[0].reshape(1, d_ff)

    mod = _ada_mod(jnp.concatenate([c_prompt, c_sample], axis=0), w_ada_bf, b_ada[0].reshape(1, 6 * d))

    mp = bp * tp
    modp = [mod[:bp, i * d:(i + 1) * d].reshape(bp, 1, d) for i in range(6)]
    xp = x_prompt.reshape(mp, d)
    (qt, kt32, vt32, kb, vtb, qh, lf, kh, hi, hg, ga, gb) = _in_proj(
        xp, modp[0], modp[1], n1, w_in_bf, hgrn_lb_logits, per_row=False, seq_len=tp)
    seq3 = lambda a: a.reshape(bp, tp, w)
    oa = _moba_prompt(qt, seq3(kb), vtb).reshape(mp, w)
    ob, sp = _hgrn(seq3(qh), seq3(lf), seq3(kh), seq3(hi), seq3(hg),
                   jnp.zeros((bp, HG_HEADS, HG_DK, HG_DV), F32), hgn)
    x1, h2 = _mix_out(oa, ob.reshape(mp, w), ga, gb, xp, modp[2], modp[3], modp[4], n2, wa_bf, wb_bf, wo_bf,
                      per_row=False, seq_len=tp)
    yp, cp = _ffn_seq(h2, x1, modp[5], jnp.zeros((bp, CONV_W - 1, d_ff), F32), wup_bf, cw, cb, wdn_bf, fg,
                      batch=bp, seq_len=tp)

    ms = bs * ts
    mods_rows = jnp.repeat(mod[bp:], ts, axis=0)
    mods = [mods_rows[:, i * d:(i + 1) * d].reshape(1, ms, d) for i in range(6)]
    xs = x_sample.reshape(ms, d)
    (q_s, k32_s, v32_s, kb_s, vb_s, qh_s, lf_s, kh_s, hi_s, hg_s, ga_s, gb_s) = _in_proj(
        xs, mods[0], mods[1], n1, w_in_bf, hgrn_lb_logits, per_row=True, seq_len=ts)
    seq3s = lambda a: a.reshape(bs, ts, w)
    n_phys = cache_k.shape[1]
    page_t = lambda c: jnp.transpose(c[0], (0, 2, 3, 1)).reshape(n_phys, w, PAGE_SIZE)
    oa_s = _moba_decode(seq3s(q_s), seq3s(kb_s), seq3s(vb_s), page_t(cache_k), page_t(cache_v), page_table)
    t_pad = HG_CHUNK * pl.cdiv(ts, HG_CHUNK)
    padt = lambda a: jnp.pad(seq3s(a), ((0, 0), (0, t_pad - ts), (0, 0)))
    ob_s, ss = _hgrn(padt(qh_s), padt(lf_s), padt(kh_s), padt(hi_s), padt(hg_s), state_hgrn[0], hgn)
    ob_s = ob_s[:, :ts, :].reshape(ms, w)
    x1_s, h2_s = _mix_out(oa_s.reshape(ms, w).astype(BF16), ob_s, ga_s, gb_s, xs, mods[2], mods[3], mods[4], n2,
                          wa_bf, wb_bf, wo_bf, per_row=True, seq_len=ts)
    ys, cs = _ffn_short(h2_s, x1_s, mods[5], state_conv[0], wup_bf, cw, cb, wdn_bf, fg, batch=bs, seq_len=ts)

    kv_p = lambda a: jnp.transpose(a.reshape(1, bp, A_HEADS, A_HD, tp), (0, 1, 4, 2, 3))
    kv_s = lambda a: a.reshape(1, bs, ts, A_HEADS, A_HD)
    return (yp.reshape(bp, tp, d), ys.reshape(bs, ts, d), kv_p(kt32), kv_p(vt32), kv_s(k32_s), kv_s(v32_s),
            sp[None], ss[None], cp[None], cs[None])
```
